```python
import jax, jax.numpy as jnp
from jax import lax
import numpy as np

D_MODEL = 4096
BATCH = 2
SEQ = 4096
DEPTH = 2

HEAD_DIM = 128
ATT_HEADS = 8
ATT_KV_HEADS = 2
ATT_WINDOW = 128
ATT_BLOCK = 128
RET_HEADS = 8
RET_CHUNK = 128
SG_GROUPS = 8
SG_CHUNK = 128
HGRN_HEADS = 8
HGRN_CHUNK = 16
N_BRANCH = 4
ATT_Q_W = ATT_HEADS * HEAD_DIM
ATT_KV_W = ATT_KV_HEADS * HEAD_DIM
RET_W = RET_HEADS * HEAD_DIM
SG_W = SG_GROUPS * HEAD_DIM
HGRN_W = HGRN_HEADS * HEAD_DIM
BRANCH_W = HEAD_DIM * 8
MIX_SPLITS = (ATT_Q_W, ATT_KV_W, ATT_KV_W, RET_W, RET_W, RET_W, RET_W, SG_W, SG_W, HGRN_W, HGRN_W, HGRN_W, HGRN_W, HGRN_W)
MIX_IN_W = sum(MIX_SPLITS)
FFN_HIDDEN = -(-8 * D_MODEL // (3 * 256)) * 256
DEEPNORM_ALPHA = (2.0 * DEPTH) ** 0.25
DEEPNORM_BETA = (8.0 * DEPTH) ** -0.25
LN_EPS = 1e-5

kernel_name = 'hybrid_gated_quad_mixer_deepnorm_encoder'


def _layer_norm(x):
    x32 = x.astype(jnp.float32)
    mu = jnp.mean(x32, axis=-1, keepdims=True)
    var = jnp.mean(jnp.square(x32 - mu), axis=-1, keepdims=True)
    return ((x32 - mu) * lax.rsqrt(var + LN_EPS)).astype(x.dtype)


def _rms_norm(x):
    x32 = x.astype(jnp.float32)
    return (x32 * lax.rsqrt(jnp.mean(jnp.square(x32), axis=-1, keepdims=True) + LN_EPS)).astype(x.dtype)


def _heads(t, n_heads):
    b, s, w = t.shape
    return t.reshape(b, s, n_heads, w // n_heads).transpose(0, 2, 1, 3)


def windowed_gqa(q, k, v, sink):
    f32 = jnp.float32
    b, s, hq, dh = q.shape
    hkv = k.shape[2]
    grp = hq // hkv
    p = ATT_BLOCK
    nb = s // p
    qb = q.reshape(b, nb, p, hkv, grp, dh)

    def band(t):
        tp = jnp.pad(t, ((0, 0), (p, p), (0, 0), (0, 0))).reshape(b, nb + 2, p, hkv, dh)
        return jnp.concatenate([tp[:, :-2], tp[:, 1:-1], tp[:, 2:]], axis=2)

    kb, vb = band(k), band(v)
    scores = jnp.einsum('bnqhgd,bnkhd->bnhgqk', qb, kb, preferred_element_type=f32) * dh ** -0.5
    qi = jnp.arange(p)[:, None]
    kj = jnp.arange(3 * p)[None, :] - p
    dist = jnp.abs(kj - qi)
    spos = jnp.arange(nb)[:, None, None] * p + kj[None]
    valid = (dist <= ATT_WINDOW)[None] & (spos >= 0) & (spos < s)
    slopes = jnp.exp2(-8.0 * jnp.arange(1, hq + 1, dtype=f32) / hq).reshape(hkv, grp)
    scores = scores - slopes[:, :, None, None] * dist.astype(f32)
    scores = jnp.where(valid[None, :, None, None], scores, -jnp.inf)
    sink_col = jnp.broadcast_to(sink.astype(f32).reshape(hkv, grp)[:, :, None, None], scores.shape[:-1] + (1,))
    probs = jax.nn.softmax(jnp.concatenate([scores, sink_col], axis=-1), axis=-1)[..., :-1]
    out = jnp.einsum('bnhgqk,bnkhd->bnqhgd', probs.astype(v.dtype), vb)
    return out.reshape(b, s, hq * dh)


def _retention_chunkwise(q, k, v, log_gamma, include_diag):
    b, h, s, dk = q.shape
    dv = v.shape[-1]
    n = s // RET_CHUNK
    qc = q.reshape(b, h, n, RET_CHUNK, dk)
    kc = k.reshape(b, h, n, RET_CHUNK, dk)
    vc = v.reshape(b, h, n, RET_CHUNK, dv)
    idx = jnp.arange(RET_CHUNK, dtype=jnp.float32)
    diff = idx[:, None] - idx[None, :]
    mask = diff >= 0 if include_diag else diff > 0
    lg = log_gamma[:, None, None]
    dmat = jnp.where(mask, jnp.exp(lg * jnp.maximum(diff, 0.0)), 0.0)
    scores = jnp.einsum('bhnad,bhncd->bhnac', qc, kc) * dmat[None, :, None]
    o_intra = jnp.einsum('bhnac,bhncv->bhnav', scores, vc)
    q_dec = jnp.exp(log_gamma[:, None] * (idx + 1.0))
    k_dec = jnp.exp(log_gamma[:, None] * (RET_CHUNK - 1.0 - idx))
    chunk_dec = jnp.exp(log_gamma * RET_CHUNK)[None, :, None, None]
    kv = jnp.einsum('bhncd,hc,bhncv->nbhdv', kc, k_dec, vc)

    def step(state, kv_j):
        return state * chunk_dec + kv_j, state

    _, prev = lax.scan(step, jnp.zeros((b, h, dk, dv), jnp.float32), kv)
    o_inter = jnp.einsum('bhnad,ha,nbhdv->bhnav', qc, q_dec, prev)
    return (o_intra + o_inter).reshape(b, h, s, dv)


def retention_mixer(rq, rk, rv, rg):
    f32 = jnp.float32
    q = _heads(rq, RET_HEADS).astype(f32)
    k = _heads(rk, RET_HEADS).astype(f32) * HEAD_DIM ** -0.5
    v = _heads(rv, RET_HEADS).astype(f32)
    log_gamma = jnp.log1p(-jnp.exp2(-5.0 - jnp.arange(RET_HEADS, dtype=f32)))
    fwd = _retention_chunkwise(q, k, v, log_gamma, True)
    bwd = _retention_chunkwise(q[:, :, ::-1], k[:, :, ::-1], v[:, :, ::-1], log_gamma, False)[:, :, ::-1]
    o = _layer_norm((fwd + bwd).transpose(0, 2, 1, 3)).reshape(rg.shape)
    return (o * jax.nn.silu(rg.astype(f32))).astype(rg.dtype)


def spatial_gating_mixer(su, sv, sg_w, sg_b):
    b, s, w = su.shape
    n = s // SG_CHUNK
    u = jax.nn.gelu(su)
    v = _layer_norm(jax.nn.gelu(sv)).reshape(b, n, SG_CHUNK, SG_GROUPS, w // SG_GROUPS)
    mixed = jnp.einsum('gts,bnsgd->bntgd', sg_w, v) + sg_b.T[None, None, :, :, None]
    return u * mixed.reshape(b, s, w)


def _hgrn2_chunkwise(q, k, v, log_f):
    b, h, s, dk = q.shape
    dv = v.shape[-1]
    n = s // HGRN_CHUNK

    def chunks(t):
        return jnp.moveaxis(t.reshape(b, h, n, HGRN_CHUNK, t.shape[-1]), 2, 0)

    tri = jnp.tril(jnp.ones((HGRN_CHUNK, HGRN_CHUNK), dtype=bool))[:, :, None]

    def step(state, xs):
        qj, kj, vj, gj = xs
        cum = jnp.cumsum(gj, axis=2)
        o_inter = jnp.einsum('bhad,bhdv->bhav', qj * jnp.exp(cum), state)
        rel = cum[:, :, :, None, :] - cum[:, :, None, :, :]
        dec = jnp.exp(jnp.where(tri, rel, -jnp.inf))
        attn = jnp.einsum('bhad,bhcd,bhacd->bhac', qj, kj, dec)
        o = o_inter + jnp.einsum('bhac,bhcv->bhav', attn, vj)
        last = cum[:, :, -1:, :]
        state = jnp.exp(last[:, :, 0, :])[..., None] * state + jnp.einsum('bhcd,bhcv->bhdv', kj * jnp.exp(last - cum), vj)
        return state, o

    _, o = lax.scan(step, jnp.zeros((b, h, dk, dv), jnp.float32), (chunks(q), chunks(k), chunks(v), chunks(log_f)))
    return jnp.moveaxis(o, 0, 2).reshape(b, h, s, dv)


def hgrn2_mixer(dq, dff, dfb, di, dg, lb):
    f32 = jnp.float32
    q = _heads(dq, HGRN_HEADS).astype(f32)
    v = _heads(di, HGRN_HEADS).astype(f32)

    def gate(z):
        f = lb + (1.0 - lb) * jax.nn.sigmoid(z.astype(f32))
        return _heads(1.0 - f, HGRN_HEADS), _heads(jnp.log(f), HGRN_HEADS)

    k_f, lf_f = gate(dff)
    k_b, lf_b = gate(dfb)
    fwd = _hgrn2_chunkwise(q, k_f, v, lf_f)
    bwd = _hgrn2_chunkwise(q[:, :, ::-1], k_b[:, :, ::-1], v[:, :, ::-1], lf_b[:, :, ::-1])[:, :, ::-1]
    o = _rms_norm((fwd + bwd).transpose(0, 2, 1, 3)).reshape(dg.shape)
    return (o * jax.nn.silu(dg.astype(f32))).astype(dg.dtype)


def _token_mixers(h, w_in, sink, sg_w, sg_b, lb, w_branch, w_gate, w_o):
    b, s, _ = h.shape
    offs = np.cumsum(MIX_SPLITS)[:-1].tolist()
    (aq, ak, av, rq, rk, rv, rg, su, sv, dq, dff, dfb, di, dg) = jnp.split(h @ w_in, offs, axis=-1)
    y_a = windowed_gqa(aq.reshape(b, s, ATT_HEADS, HEAD_DIM), ak.reshape(b, s, ATT_KV_HEADS, HEAD_DIM),
                       av.reshape(b, s, ATT_KV_HEADS, HEAD_DIM), sink)
    y_b = retention_mixer(rq, rk, rv, rg)
    y_c = spatial_gating_mixer(su, sv, sg_w, sg_b)
    y_d = hgrn2_mixer(dq, dff, dfb, di, dg, lb)
    merged = None
    for i, y in enumerate((y_a, y_b, y_c, y_d)):
        term = jax.nn.sigmoid(h @ w_gate[i]) * (y @ w_branch[i])
        merged = term if merged is None else merged + term
    return merged @ w_o


def setup_inputs(seed: int = 0) -> dict:
    key = jax.random.key(seed)
    ks = jax.random.split(key, 16)
    f32 = jnp.float32

    def nrm(k, shape, scale):
        return jax.random.normal(k, shape, f32) * scale

    return {
        'x': nrm(ks[0], (BATCH, SEQ, D_MODEL), 1.0),
        'c': nrm(ks[1], (BATCH, D_MODEL), 1.0),
        'w_in': nrm(ks[2], (DEPTH, D_MODEL, MIX_IN_W), D_MODEL ** -0.5),
        'attn_sink': nrm(ks[3], (DEPTH, ATT_HEADS), 0.5),
        'sg_w': nrm(ks[4], (DEPTH, SG_GROUPS, SG_CHUNK, SG_CHUNK), SG_CHUNK ** -0.5),
        'sg_b': 1.0 + nrm(ks[5], (DEPTH, SG_GROUPS, SG_CHUNK), 0.02),
        'hgrn_lb_logits': nrm(ks[6], (DEPTH, HGRN_W), 0.5),
        'w_branch': nrm(ks[7], (DEPTH, N_BRANCH, BRANCH_W, D_MODEL), BRANCH_W ** -0.5),
        'w_gate': nrm(ks[8], (DEPTH, N_BRANCH, D_MODEL, D_MODEL), D_MODEL ** -0.5),
        'w_o': nrm(ks[9], (DEPTH, D_MODEL, D_MODEL), DEEPNORM_BETA * D_MODEL ** -0.5),
        'w_mod': nrm(ks[10], (DEPTH, D_MODEL, 6 * D_MODEL), 0.1 * D_MODEL ** -0.5),
        'b_mod': nrm(ks[11], (DEPTH, 6 * D_MODEL), 0.01),
        'ln_g': 1.0 + nrm(ks[12], (DEPTH, 2, D_MODEL), 0.02),
        'ln_b': nrm(ks[13], (DEPTH, 2, D_MODEL), 0.02),
        'w_ffn_in': nrm(ks[14], (DEPTH, D_MODEL, 2 * FFN_HIDDEN), D_MODEL ** -0.5),
        'w_ffn_out': nrm(ks[15], (DEPTH, FFN_HIDDEN, D_MODEL), DEEPNORM_BETA * FFN_HIDDEN ** -0.5),
    }


def reference(x, c, w_in, attn_sink, sg_w, sg_b, hgrn_lb_logits, w_branch, w_gate, w_o, w_mod, b_mod, ln_g, ln_b, w_ffn_in, w_ffn_out):
    p = jax.nn.softmax(hgrn_lb_logits.astype(jnp.float32), axis=0)
    lower_bounds = jnp.cumsum(p, axis=0) - p[:1]
    c_act = jax.nn.silu(c)
    for l in range(DEPTH):
        mod = (c_act @ w_mod[l] + b_mod[l])[:, None, :]
        sh_m, sc_m, g_m, sh_f, sc_f, g_f = jnp.split(mod, 6, axis=-1)
        h = x * (1.0 + sc_m) + sh_m
        m = _token_mixers(h, w_in[l], attn_sink[l], sg_w[l], sg_b[l], lower_bounds[l], w_branch[l], w_gate[l], w_o[l])
        x = _layer_norm(DEEPNORM_ALPHA * x + (1.0 + g_m) * m) * ln_g[l, 0] + ln_b[l, 0]
        h = x * (1.0 + sc_f) + sh_f
        a, g = jnp.split(h @ w_ffn_in[l], 2, axis=-1)
        f = (jax.nn.silu(a) * g) @ w_ffn_out[l]
        x = _layer_norm(DEEPNORM_ALPHA * x + (1.0 + g_f) * f) * ln_g[l, 1] + ln_b[l, 1]
    return x
```

```python
import functools

import jax
import jax.numpy as jnp
from jax import lax
from jax.experimental import pallas as pl
from jax.experimental.pallas import tpu as pltpu

F32 = jnp.float32
BF16 = jnp.bfloat16

HEAD_DIM = 128
ATT_HEADS = 8
ATT_KV_HEADS = 2
ATT_GROUP = ATT_HEADS // ATT_KV_HEADS
ATT_WINDOW = 128
RET_HEADS = 8
SG_GROUPS = 8
HGRN_HEADS = 8
CHUNK = 128
SUB = 16
BRANCH_W = 8 * HEAD_DIM
LN_EPS = 1e-5

_OFF_AQ = 0
_OFF_AK = _OFF_AQ + ATT_HEADS
_OFF_AV = _OFF_AK + ATT_KV_HEADS
_OFF_RQ = _OFF_AV + ATT_KV_HEADS
_OFF_RK = _OFF_RQ + RET_HEADS
_OFF_RV = _OFF_RK + RET_HEADS
_OFF_RG = _OFF_RV + RET_HEADS
_OFF_SU = _OFF_RG + RET_HEADS
_OFF_SV = _OFF_SU + SG_GROUPS
_OFF_DQ = _OFF_SV + SG_GROUPS
_OFF_DFF = _OFF_DQ + HGRN_HEADS
_OFF_DFB = _OFF_DFF + HGRN_HEADS
_OFF_DI = _OFF_DFB + HGRN_HEADS
_OFF_DG = _OFF_DI + HGRN_HEADS

VMEM_LIMIT_BYTES_V7X = 56 * 1024 * 1024


def _params(n_axes):
    return pltpu.CompilerParams(dimension_semantics=("arbitrary",) * n_axes,
                                vmem_limit_bytes=VMEM_LIMIT_BYTES_V7X)


def _dot(a, b):
    return jnp.dot(a, b, preferred_element_type=F32)


def _dot_nt(a, b):
    return lax.dot_general(a, b, (((1,), (1,)), ((), ())), preferred_element_type=F32)


def _dot_tn(a, b):
    return lax.dot_general(a, b, (((0,), (0,)), ((), ())), preferred_element_type=F32)


def _sigmoid(x):
    return 1.0 / (1.0 + jnp.exp(-x))


def _silu(x):
    return x * _sigmoid(x)


def _gelu_tanh(x):
    return 0.5 * x * (1.0 + jnp.tanh(0.7978845608028654 * (x + 0.044715 * (x * x * x))))


def _pick(n, prefs):
    for p in prefs:
        if n % p == 0:
            return p
    return n


def _mod_kernel(c_ref, w_ref, b_ref, o_ref):
    ca = _silu(c_ref[...]).astype(BF16)
    o_ref[...] = _dot(ca, w_ref[...].astype(BF16)) + b_ref[...]


def _mod_call(c_pad, w_mod, b_mod):
    depth, d, n = w_mod.shape
    rows = c_pad.shape[0]
    tn = _pick(n, (512, 256, 128))
    return pl.pallas_call(
        _mod_kernel,
        out_shape=jax.ShapeDtypeStruct((depth, rows, n), F32),
        grid=(depth, n // tn),
        in_specs=[pl.BlockSpec((rows, d), lambda l, j: (0, 0)),
                  pl.BlockSpec((None, d, tn), lambda l, j: (l, 0, j)),
                  pl.BlockSpec((None, 1, tn), lambda l, j: (l, 0, j))],
        out_specs=pl.BlockSpec((None, rows, tn), lambda l, j: (l, 0, j)),
        compiler_params=_params(2),
        name="adaln_mod",
    )(c_pad, w_mod, b_mod.reshape(depth, 1, n))


def _modulate_kernel(x_ref, sc_ref, sh_ref, o_ref):
    o_ref[...] = (x_ref[...] * (1.0 + sc_ref[...]) + sh_ref[...]).astype(o_ref.dtype)


def _modulate_call(x, sc, sh):
    b, s, d = x.shape
    ts = _pick(s, (512, 256, 128))
    return pl.pallas_call(
        _modulate_kernel,
        out_shape=jax.ShapeDtypeStruct((b, s, d), BF16),
        grid=(b, s // ts),
        in_specs=[pl.BlockSpec((None, ts, d), lambda i, j: (i, j, 0)),
                  pl.BlockSpec((None, 1, d), lambda i, j: (i, 0, 0)),
                  pl.BlockSpec((None, 1, d), lambda i, j: (i, 0, 0))],
        out_specs=pl.BlockSpec((None, ts, d), lambda i, j: (i, j, 0)),
        compiler_params=_params(2),
        name="modulate",
    )(x, sc, sh)


def _mm_kernel(x_ref, w_ref, o_ref):
    o_ref[...] = _dot(x_ref[...], w_ref[...]).astype(o_ref.dtype)


def _matmul(x, w, *, tm, tn, out_dtype, name):
    m, k = x.shape
    n = w.shape[1]
    return pl.pallas_call(
        _mm_kernel,
        out_shape=jax.ShapeDtypeStruct((m, n), out_dtype),
        grid=(m // tm, n // tn),
        in_specs=[pl.BlockSpec((tm, k), lambda i, j: (i, 0)),
                  pl.BlockSpec((k, tn), lambda i, j: (0, j))],
        out_specs=pl.BlockSpec((tm, tn), lambda i, j: (i, j)),
        compiler_params=_params(2),
        name=name,
    )(x, w)


def _merge_kernel(h_ref, ya_ref, yb_ref, yc_ref, yd_ref, wg_ref, wb_ref, o_ref):
    h = h_ref[...]
    acc = None
    for i, y_ref in enumerate((ya_ref, yb_ref, yc_ref, yd_ref)):
        gate = _sigmoid(_dot(h, wg_ref[i]))
        term = gate * _dot(y_ref[...], wb_ref[i])
        acc = term if acc is None else acc + term
    o_ref[...] = acc.astype(o_ref.dtype)


def _merge_call(h, ys, wg, wb):
    m, d = h.shape
    bw = ys[0].shape[1]
    n = wg.shape[2]
    tm = _pick(m, (512, 256, 128))
    tn = _pick(n, (256, 128))
    y_spec = pl.BlockSpec((tm, bw), lambda j, i: (i, 0))
    return pl.pallas_call(
        _merge_kernel,
        out_shape=jax.ShapeDtypeStruct((m, n), BF16),
        grid=(n // tn, m // tm),
        in_specs=[pl.BlockSpec((tm, d), lambda j, i: (i, 0)), y_spec, y_spec, y_spec, y_spec,
                  pl.BlockSpec((4, d, tn), lambda j, i: (0, 0, j)),
                  pl.BlockSpec((4, bw, tn), lambda j, i: (0, 0, j))],
        out_specs=pl.BlockSpec((tm, tn), lambda j, i: (i, j)),
        compiler_params=_params(2),
        name="gated_merge",
    )(h, *ys, wg, wb)


def _ln_kernel(x_ref, m_ref, gate_ref, lg_ref, lb_ref, sc_ref, sh_ref, xo_ref, ho_ref, *, alpha):
    z = alpha * x_ref[...] + (1.0 + gate_ref[...]) * m_ref[...]
    mu = jnp.mean(z, axis=-1, keepdims=True)
    zc = z - mu
    var = jnp.mean(zc * zc, axis=-1, keepdims=True)
    xn = zc * lax.rsqrt(var + LN_EPS) * lg_ref[...] + lb_ref[...]
    xo_ref[...] = xn
    ho_ref[...] = (xn * (1.0 + sc_ref[...]) + sh_ref[...]).astype(ho_ref.dtype)


def _ln_call(x, m, gate, ln_g, ln_b, sc, sh, alpha):
    b, s, d = x.shape
    ts = _pick(s, (256, 128))
    big = pl.BlockSpec((None, ts, d), lambda i, j: (i, j, 0))
    per_b = pl.BlockSpec((None, 1, d), lambda i, j: (i, 0, 0))
    shared = pl.BlockSpec((1, d), lambda i, j: (0, 0))
    return pl.pallas_call(
        functools.partial(_ln_kernel, alpha=alpha),
        out_shape=(jax.ShapeDtypeStruct((b, s, d), F32), jax.ShapeDtypeStruct((b, s, d), BF16)),
        grid=(b, s // ts),
        in_specs=[big, big, per_b, shared, shared, per_b, per_b],
        out_specs=(big, big),
        compiler_params=_params(2),
        name="residual_ln",
    )(x, m, gate, ln_g, ln_b, sc, sh)


def _ffn_in_kernel(h_ref, wa_ref, wg_ref, o_ref):
    h = h_ref[...]
    a = _dot(h, wa_ref[...])
    g = _dot(h, wg_ref[...])
    o_ref[...] = (_silu(a) * g).astype(o_ref.dtype)


def _ffn_in_call(h, w):
    m, d = h.shape
    f = w.shape[1] // 2
    tm = _pick(m, (1024, 512, 256, 128))
    tn = _pick(f, (256, 128))
    nj = f // tn
    return pl.pallas_call(
        _ffn_in_kernel,
        out_shape=jax.ShapeDtypeStruct((m, f), BF16),
        grid=(m // tm, nj),
        in_specs=[pl.BlockSpec((tm, d), lambda i, j: (i, 0)),
                  pl.BlockSpec((d, tn), lambda i, j: (0, j)),
                  pl.BlockSpec((d, tn), lambda i, j: (0, j + nj))],
        out_specs=pl.BlockSpec((tm, tn), lambda i, j: (i, j)),
        compiler_params=_params(2),
        name="swiglu_in",
    )(h, w, w)


def _attn_kernel(slope_ref, sink_ref, q_ref, k_ref, v_ref, o_ref, *, seq):
    hkv = pl.program_id(1)
    span = 3 * CHUNK
    scale = HEAD_DIM ** -0.5

    def block(n, carry):
        q0 = pl.multiple_of(n * CHUNK, CHUNK)
        ks = pl.multiple_of(jnp.clip(q0 - CHUNK, 0, seq - span), CHUNK)
        kb = k_ref[pl.ds(ks, span), :].astype(BF16)
        vb = v_ref[pl.ds(ks, span), :].astype(BF16)
        qpos = q0 + lax.broadcasted_iota(jnp.int32, (CHUNK, span), 0)
        kpos = ks + lax.broadcasted_iota(jnp.int32, (CHUNK, span), 1)
        dist = jnp.abs(qpos - kpos)
        valid = dist <= ATT_WINDOW
        distf = dist.astype(F32)
        for g in range(ATT_GROUP):
            head = hkv * ATT_GROUP + g
            qg = q_ref[pl.ds(q0, CHUNK), g * HEAD_DIM:(g + 1) * HEAD_DIM].astype(BF16)
            sc = _dot_nt(qg, kb) * scale - slope_ref[head] * distf
            sc = jnp.where(valid, sc, -jnp.inf)
            sink = sink_ref[head]
            mx = jnp.maximum(jnp.max(sc, axis=-1, keepdims=True), sink)
            p = jnp.exp(sc - mx)
            den = jnp.sum(p, axis=-1, keepdims=True) + jnp.exp(sink - mx)
            o = _dot(p.astype(BF16), vb) / den
            o_ref[pl.ds(q0, CHUNK), g * HEAD_DIM:(g + 1) * HEAD_DIM] = o.astype(o_ref.dtype)
        return carry

    lax.fori_loop(0, seq // CHUNK, block, 0)


def _attn_call(proj, slopes, sink):
    b, s, _ = proj.shape
    qw = ATT_GROUP * HEAD_DIM
    smem = pl.BlockSpec(memory_space=pltpu.SMEM)
    return pl.pallas_call(
        functools.partial(_attn_kernel, seq=s),
        out_shape=jax.ShapeDtypeStruct((b, s, BRANCH_W), BF16),
        grid=(b, ATT_KV_HEADS),
        in_specs=[smem, smem,
                  pl.BlockSpec((None, s, qw), lambda i, h: (i, 0, h)),
                  pl.BlockSpec((None, s, HEAD_DIM), lambda i, h: (i, 0, _OFF_AK + h)),
                  pl.BlockSpec((None, s, HEAD_DIM), lambda i, h: (i, 0, _OFF_AV + h))],
        out_specs=pl.BlockSpec((None, s, qw), lambda i, h: (i, 0, h)),
        compiler_params=_params(2),
        name="windowed_gqa",
    )(slopes, sink, proj, proj, proj)


def _ret_kernel(lg_ref, q_ref, k_ref, v_ref, g_ref, o_ref, sf_ref, *, seq):
    head = pl.program_id(1)
    lg = lg_ref[head]
    nc = seq // CHUNK
    c = CHUNK
    row = lax.broadcasted_iota(jnp.int32, (c, c), 0)
    col = lax.broadcasted_iota(jnp.int32, (c, c), 1)
    dmat = jnp.exp(lg * jnp.abs(row - col).astype(F32))
    pos = lax.broadcasted_iota(jnp.int32, (c, 1), 0).astype(F32)
    kdec_f = jnp.exp(lg * (c - 1.0 - pos))
    kdec_b = jnp.exp(lg * pos)
    qdec_f = jnp.exp(lg * (pos + 1.0))
    qdec_b = jnp.exp(lg * (c - pos))
    cdec = jnp.exp(lg * c)
    kscale = HEAD_DIM ** -0.5

    def fwd(n, st):
        r0 = pl.multiple_of(n * c, c)
        sf_ref[n] = st
        kc = k_ref[pl.ds(r0, c), :] * kscale
        vc = v_ref[pl.ds(r0, c), :].astype(BF16)
        return st * cdec + _dot_tn((kc * kdec_f).astype(BF16), vc)

    lax.fori_loop(0, nc, fwd, jnp.zeros((c, c), F32))

    def bwd(i, st):
        n = nc - 1 - i
        r0 = pl.multiple_of(n * c, c)
        qc = q_ref[pl.ds(r0, c), :]
        kc = k_ref[pl.ds(r0, c), :] * kscale
        vc = v_ref[pl.ds(r0, c), :].astype(BF16)
        p = _dot_nt(qc.astype(BF16), kc.astype(BF16)) * dmat
        o = _dot(p.astype(BF16), vc)
        o = o + _dot((qc * qdec_f).astype(BF16), sf_ref[n].astype(BF16))
        o = o + _dot((qc * qdec_b).astype(BF16), st.astype(BF16))
        mu = jnp.mean(o, axis=-1, keepdims=True)
        oc = o - mu
        var = jnp.mean(oc * oc, axis=-1, keepdims=True)
        y = oc * lax.rsqrt(var + LN_EPS) * _silu(g_ref[pl.ds(r0, c), :])
        o_ref[pl.ds(r0, c), :] = y.astype(o_ref.dtype)
        return st * cdec + _dot_tn((kc * kdec_b).astype(BF16), vc)

    lax.fori_loop(0, nc, bwd, jnp.zeros((c, c), F32))


def _ret_call(proj, log_gamma):
    b, s, _ = proj.shape

    def col(off):
        return pl.BlockSpec((None, s, HEAD_DIM), lambda i, h: (i, 0, off + h))

    return pl.pallas_call(
        functools.partial(_ret_kernel, seq=s),
        out_shape=jax.ShapeDtypeStruct((b, s, BRANCH_W), BF16),
        grid=(b, RET_HEADS),
        in_specs=[pl.BlockSpec(memory_space=pltpu.SMEM), col(_OFF_RQ), col(_OFF_RK), col(_OFF_RV), col(_OFF_RG)],
        out_specs=pl.BlockSpec((None, s, HEAD_DIM), lambda i, h: (i, 0, h)),
        scratch_shapes=[pltpu.VMEM((s // CHUNK, CHUNK, CHUNK), F32)],
        compiler_params=_params(2),
        name="retention",
    )(log_gamma, proj, proj, proj, proj)


def _sgu_kernel(u0_ref, u1_ref, v0_ref, v1_ref, w_ref, b_ref, o_ref):
    half = u0_ref.shape[-1]
    v = jnp.concatenate([_gelu_tanh(v0_ref[...]), _gelu_tanh(v1_ref[...])], axis=-1)
    mu = jnp.mean(v, axis=-1, keepdims=True)
    vc = v - mu
    var = jnp.mean(vc * vc, axis=-1, keepdims=True)
    vn = (vc * lax.rsqrt(var + LN_EPS)).astype(BF16)
    for g in range(SG_GROUPS):
        lo = g * HEAD_DIM
        mixed = _dot(w_ref[g], vn[:, lo:lo + HEAD_DIM]) + b_ref[:, g:g + 1]
        u_ref = u0_ref if lo < half else u1_ref
        ul = lo % half
        u = _gelu_tanh(u_ref[:, ul:ul + HEAD_DIM])
        o_ref[:, lo:lo + HEAD_DIM] = (u * mixed).astype(o_ref.dtype)


def _sgu_call(proj, sg_w, sg_bt):
    b, s, _ = proj.shape
    half = SG_GROUPS * HEAD_DIM // 2
    hb = half // HEAD_DIM

    def blk(off):
        return pl.BlockSpec((None, CHUNK, half), lambda i, n: (i, n, off))

    return pl.pallas_call(
        _sgu_kernel,
        out_shape=jax.ShapeDtypeStruct((b, s, BRANCH_W), BF16),
        grid=(b, s // CHUNK),
        in_specs=[blk(_OFF_SU // hb), blk(_OFF_SU // hb + 1), blk(_OFF_SV // hb), blk(_OFF_SV // hb + 1),
                  pl.BlockSpec((SG_GROUPS, CHUNK, CHUNK), lambda i, n: (0, 0, 0)),
                  pl.BlockSpec((CHUNK, SG_GROUPS), lambda i, n: (0, 0))],
        out_specs=pl.BlockSpec((None, CHUNK, BRANCH_W), lambda i, n: (i, n, 0)),
        compiler_params=_params(2),
        name="spatial_gating",
    )(proj, proj, proj, proj, sg_w, sg_bt)


def _split3(x):
    x1 = x.astype(BF16)
    r1 = x - x1.astype(F32)
    x2 = r1.astype(BF16)
    x3 = (r1 - x2.astype(F32)).astype(BF16)
    return x1, x2, x3


def _hgrn_chunk(q, z, v, lb, st, forward):
    c = CHUNK
    f = lb + (1.0 - lb) * _sigmoid(z)
    kk = 1.0 - f
    logf = jnp.log(f)
    row = lax.broadcasted_iota(jnp.int32, (c, c), 0)
    col = lax.broadcasted_iota(jnp.int32, (c, c), 1)
    allowed = (col <= row) if forward else (col >= row)

    lmat = jnp.where(allowed, 1.0, 0.0)
    mats = [lmat]
    levels = []
    b = c // 2
    while b >= SUB:
        blk = (row // (2 * b)) * (2 * b)
        mid = blk + (b - 1 if forward else b)
        mmat = jnp.where((col <= mid) if forward else (col >= mid), 1.0, 0.0)
        mats.append(lmat - mmat)
        levels.append(b)
        b //= 2
    stacked = jnp.concatenate(mats, axis=0).astype(BF16)
    g1, g2, g3 = _split3(logf)
    sums = _dot(stacked, g1) + _dot(stacked, g2) + _dot(stacked, g3)
    a = sums[0:c]
    a_end = a[c - 1:c] if forward else a[0:1]

    vb = v.astype(BF16)
    o = _dot_nt((q * jnp.exp(a)).astype(BF16), st.astype(BF16))
    st_new = st * jnp.exp(a_end) + _dot_tn(vb, (kk * jnp.exp(a_end - a)).astype(BF16))

    rowv = lax.broadcasted_iota(jnp.int32, (c, 1), 0)
    p = jnp.zeros((c, c), F32)
    for li, b in enumerate(levels):
        d = sums[(li + 1) * c:(li + 2) * c]
        upper = (rowv % (2 * b)) >= b
        q_side = upper if forward else jnp.logical_not(upper)
        e = jnp.exp(jnp.where(q_side, d, -d))
        qt = jnp.where(q_side, q * e, 0.0).astype(BF16)
        kt = jnp.where(q_side, 0.0, kk * e).astype(BF16)
        pl_ = _dot_nt(qt, kt)
        if 2 * b < c:
            pl_ = jnp.where((row // (2 * b)) == (col // (2 * b)), pl_, 0.0)
        p = p + pl_

    rsub = rowv % SUB
    for delta in range(SUB):
        if delta == 0:
            w = q * kk
        else:
            sh = delta if forward else c - delta
            a_s = pltpu.roll(a, sh, 0)
            k_s = pltpu.roll(kk, sh, 0)
            w = q * k_s * jnp.exp(jnp.minimum(a - a_s, 0.0))
        rs = jnp.sum(w, axis=-1, keepdims=True)
        if forward:
            hit = (col == row - delta) & (rsub >= delta)
        else:
            hit = (col == row + delta) & (rsub + delta < SUB)
        p = jnp.where(hit, rs, p)

    o = o + _dot(p.astype(BF16), vb)
    return o, st_new


def _hgrn_kernel(q_ref, zf_ref, zb_ref, v_ref, g_ref, lb_ref, o_ref, of_ref, *, seq):
    nc = seq // CHUNK
    c = CHUNK
    lb = lb_ref[...]

    def fwd(n, st):
        r0 = pl.multiple_of(n * c, c)
        o, st = _hgrn_chunk(q_ref[pl.ds(r0, c), :], zf_ref[pl.ds(r0, c), :], v_ref[pl.ds(r0, c), :], lb, st, True)
        of_ref[pl.ds(r0, c), :] = o
        return st

    lax.fori_loop(0, nc, fwd, jnp.zeros((c, c), F32))

    def bwd(i, st):
        n = nc - 1 - i
        r0 = pl.multiple_of(n * c, c)
        o, st = _hgrn_chunk(q_ref[pl.ds(r0, c), :], zb_ref[pl.ds(r0, c), :], v_ref[pl.ds(r0, c), :], lb, st, False)
        o = o + of_ref[pl.ds(r0, c), :]
        y = o * lax.rsqrt(jnp.mean(o * o, axis=-1, keepdims=True) + LN_EPS) * _silu(g_ref[pl.ds(r0, c), :])
        o_ref[pl.ds(r0, c), :] = y.astype(o_ref.dtype)
        return st

    lax.fori_loop(0, nc, bwd, jnp.zeros((c, c), F32))


def _hgrn_call(proj, lb):
    b, s, _ = proj.shape

    def col(off):
        return pl.BlockSpec((None, s, HEAD_DIM), lambda i, h: (i, 0, off + h))

    return pl.pallas_call(
        functools.partial(_hgrn_kernel, seq=s),
        out_shape=jax.ShapeDtypeStruct((b, s, BRANCH_W), BF16),
        grid=(b, HGRN_HEADS),
        in_specs=[col(_OFF_DQ), col(_OFF_DFF), col(_OFF_DFB), col(_OFF_DI), col(_OFF_DG),
                  pl.BlockSpec((1, HEAD_DIM), lambda i, h: (0, h))],
        out_specs=pl.BlockSpec((None, s, HEAD_DIM), lambda i, h: (i, 0, h)),
        scratch_shapes=[pltpu.VMEM((s, HEAD_DIM), F32)],
        compiler_params=_params(2),
        name="hgrn2",
    )(proj, proj, proj, proj, proj, lb)


def kernel(x, c, w_in, attn_sink, sg_w, sg_b, hgrn_lb_logits, w_branch, w_gate, w_o, w_mod, b_mod, ln_g, ln_b, w_ffn_in, w_ffn_out):
    bsz, seq, d = x.shape
    depth = w_in.shape[0]
    m = bsz * seq
    alpha = (2.0 * depth) ** 0.25

    pz = jax.nn.softmax(hgrn_lb_logits.astype(F32), axis=0)
    lower_bounds = jnp.cumsum(pz, axis=0) - pz[:1]
    slopes = jnp.exp2(-8.0 * jnp.arange(1, ATT_HEADS + 1, dtype=F32) / ATT_HEADS)
    log_gamma = jnp.log1p(-jnp.exp2(-5.0 - jnp.arange(RET_HEADS, dtype=F32)))

    rows = 8
    c_pad = jnp.zeros((rows, d), F32).at[:bsz].set(c)
    mod = _mod_call(c_pad, w_mod, b_mod)[:, :bsz]

    def mod_part(l, i):
        return mod[l, :, i * d:(i + 1) * d].reshape(bsz, 1, d)

    tm = _pick(m, (1024, 512, 256, 128))
    h = _modulate_call(x, mod_part(0, 1), mod_part(0, 0))
    for l in range(depth):
        h2d = h.reshape(m, d)
        proj = _matmul(h2d, w_in[l].astype(BF16), tm=tm, tn=_pick(w_in.shape[2], (512, 256, 128)),
                       out_dtype=F32, name="in_proj").reshape(bsz, seq, -1)
        y_a = _attn_call(proj, slopes, attn_sink[l].astype(F32))
        y_b = _ret_call(proj, log_gamma)
        y_c = _sgu_call(proj, sg_w[l].astype(BF16), sg_b[l].T)
        y_d = _hgrn_call(proj, lower_bounds[l].reshape(1, -1))
        ys = [y.reshape(m, BRANCH_W) for y in (y_a, y_b, y_c, y_d)]
        merged = _merge_call(h2d, ys, w_gate[l].astype(BF16), w_branch[l].astype(BF16))
        mix = _matmul(merged, w_o[l].astype(BF16), tm=tm, tn=_pick(d, (512, 256, 128)), out_dtype=F32,
                      name="out_proj").reshape(bsz, seq, d)
        x, h = _ln_call(x, mix, mod_part(l, 2), ln_g[l, 0:1], ln_b[l, 0:1], mod_part(l, 4), mod_part(l, 3), alpha)
        act = _ffn_in_call(h.reshape(m, d), w_ffn_in[l].astype(BF16))
        ffn = _matmul(act, w_ffn_out[l].astype(BF16), tm=_pick(m, (512, 256, 128)), tn=_pick(d, (512, 256, 128)),
                      out_dtype=F32, name="ffn_out").reshape(bsz, seq, d)
        nl = min(l + 1, depth - 1)
        x, h = _ln_call(x, ffn, mod_part(l, 5), ln_g[l, 1:2], ln_b[l, 1:2], mod_part(nl, 1), mod_part(nl, 0), alpha)
    return x
```

```python
import functools

import jax
import jax.numpy as jnp
from jax import lax
from jax.experimental import pallas as pl
from jax.experimental.pallas import tpu as pltpu

F32 = jnp.float32
BF16 = jnp.bfloat16

HEAD_DIM = 128
ATT_HEADS = 8
ATT_KV_HEADS = 2
ATT_GROUP = ATT_HEADS // ATT_KV_HEADS
ATT_WINDOW = 128
RET_HEADS = 8
SG_GROUPS = 8
HGRN_HEADS = 8
CHUNK = 128
SUB = 16
BRANCH_W = 8 * HEAD_DIM
LN_EPS = 1e-5

_OFF_AQ = 0
_OFF_AK = _OFF_AQ + ATT_HEADS
_OFF_AV = _OFF_AK + ATT_KV_HEADS
_OFF_RQ = _OFF_AV + ATT_KV_HEADS
_OFF_RK = _OFF_RQ + RET_HEADS
_OFF_RV = _OFF_RK + RET_HEADS
_OFF_RG = _OFF_RV + RET_HEADS
_OFF_SU = _OFF_RG + RET_HEADS
_OFF_SV = _OFF_SU + SG_GROUPS
_OFF_DQ = _OFF_SV + SG_GROUPS
_OFF_DFF = _OFF_DQ + HGRN_HEADS
_OFF_DFB = _OFF_DFF + HGRN_HEADS
_OFF_DI = _OFF_DFB + HGRN_HEADS
_OFF_DG = _OFF_DI + HGRN_HEADS

VMEM_LIMIT_BYTES_V7X = 56 * 1024 * 1024


def _params(n_axes):
    return pltpu.CompilerParams(dimension_semantics=("arbitrary",) * n_axes,
                                vmem_limit_bytes=VMEM_LIMIT_BYTES_V7X)


def _dot(a, b):
    return jnp.dot(a, b, preferred_element_type=F32)


def _dot_nt(a, b):
    return lax.dot_general(a, b, (((1,), (1,)), ((), ())), preferred_element_type=F32)


def _dot_tn(a, b):
    return lax.dot_general(a, b, (((0,), (0,)), ((), ())), preferred_element_type=F32)


def _sigmoid(x):
    return 1.0 / (1.0 + jnp.exp(-x))


def _silu(x):
    return x * _sigmoid(x)


def _gelu_tanh(x):
    return 0.5 * x * (1.0 + jnp.tanh(0.7978845608028654 * (x + 0.044715 * (x * x * x))))


def _pick(n, prefs):
    for p in prefs:
        if n % p == 0:
            return p
    return n


def _mod_kernel(c_ref, w_ref, b_ref, o_ref):
    ca = _silu(c_ref[...]).astype(BF16)
    o_ref[...] = _dot(ca, w_ref[...].astype(BF16)) + b_ref[...]


def _mod_call(c_pad, w_mod, b_mod):
    depth, d, n = w_mod.shape
    rows = c_pad.shape[0]
    tn = _pick(n, (512, 256, 128))
    return pl.pallas_call(
        _mod_kernel,
        out_shape=jax.ShapeDtypeStruct((depth, rows, n), F32),
        grid=(depth, n // tn),
        in_specs=[pl.BlockSpec((rows, d), lambda l, j: (0, 0)),
                  pl.BlockSpec((None, d, tn), lambda l, j: (l, 0, j)),
                  pl.BlockSpec((None, 1, tn), lambda l, j: (l, 0, j))],
        out_specs=pl.BlockSpec((None, rows, tn), lambda l, j: (l, 0, j)),
        compiler_params=_params(2),
        name="adaln_mod",
    )(c_pad, w_mod, b_mod.reshape(depth, 1, n))


def _modulate_kernel(x_ref, sc_ref, sh_ref, o_ref):
    o_ref[...] = (x_ref[...] * (1.0 + sc_ref[...]) + sh_ref[...]).astype(o_ref.dtype)


def _modulate_call(x, sc, sh):
    b, s, d = x.shape
    ts = _pick(s, (512, 256, 128))
    return pl.pallas_call(
        _modulate_kernel,
        out_shape=jax.ShapeDtypeStruct((b, s, d), BF16),
        grid=(b, s // ts),
        in_specs=[pl.BlockSpec((None, ts, d), lambda i, j: (i, j, 0)),
                  pl.BlockSpec((None, 1, d), lambda i, j: (i, 0, 0)),
                  pl.BlockSpec((None, 1, d), lambda i, j: (i, 0, 0))],
        out_specs=pl.BlockSpec((None, ts, d), lambda i, j: (i, j, 0)),
        compiler_params=_params(2),
        name="modulate",
    )(x, sc, sh)


def _mm_kernel(x_ref, w_ref, o_ref):
    o_ref[...] = _dot(x_ref[...], w_ref[...]).astype(o_ref.dtype)


def _matmul(x, w, layer, *, tm, tn, out_dtype, name):
    m, k = x.shape
    n = w.shape[2]
    return pl.pallas_call(
        _mm_kernel,
        out_shape=jax.ShapeDtypeStruct((m, n), out_dtype),
        grid=(m // tm, n // tn),
        in_specs=[pl.BlockSpec((tm, k), lambda i, j: (i, 0)),
                  pl.BlockSpec((None, k, tn), lambda i, j: (layer, 0, j))],
        out_specs=pl.BlockSpec((tm, tn), lambda i, j: (i, j)),
        compiler_params=_params(2),
        name=name,
    )(x, w)


def _mm_ws_kernel(x_ref, w_ref, o_ref, wbf_ref):
    @pl.when(pl.program_id(1) == 0)
    def _():
        wbf_ref[...] = w_ref[...].astype(BF16)

    o_ref[...] = _dot(x_ref[...], wbf_ref[...]).astype(o_ref.dtype)


def _matmul_ws(x, w, layer, *, tm, tn, out_dtype, name):
    m, k = x.shape
    n = w.shape[2]
    return pl.pallas_call(
        _mm_ws_kernel,
        out_shape=jax.ShapeDtypeStruct((m, n), out_dtype),
        grid=(n // tn, m // tm),
        in_specs=[pl.BlockSpec((tm, k), lambda j, i: (i, 0)),
                  pl.BlockSpec((None, k, tn), lambda j, i: (layer, 0, j))],
        out_specs=pl.BlockSpec((tm, tn), lambda j, i: (i, j)),
        scratch_shapes=[pltpu.VMEM((k, tn), BF16)],
        compiler_params=_params(2),
        name=name,
    )(x, w)


def _merge_kernel(h_ref, ya_ref, yb_ref, yc_ref, yd_ref, wg_ref, wb_ref, o_ref):
    h = h_ref[...]
    acc = None
    for i, y_ref in enumerate((ya_ref, yb_ref, yc_ref, yd_ref)):
        gate = _sigmoid(_dot(h, wg_ref[i]))
        term = gate * _dot(y_ref[...], wb_ref[i])
        acc = term if acc is None else acc + term
    o_ref[...] = acc.astype(o_ref.dtype)


def _merge_call(h, ys, wg, wb, layer):
    m, d = h.shape
    bw = ys[0].shape[1]
    n = wg.shape[3]
    tm = _pick(m, (512, 256, 128))
    tn = _pick(n, (256, 128))
    y_spec = pl.BlockSpec((tm, bw), lambda j, i: (i, 0))
    return pl.pallas_call(
        _merge_kernel,
        out_shape=jax.ShapeDtypeStruct((m, n), BF16),
        grid=(n // tn, m // tm),
        in_specs=[pl.BlockSpec((tm, d), lambda j, i: (i, 0)), y_spec, y_spec, y_spec, y_spec,
                  pl.BlockSpec((None, 4, d, tn), lambda j, i: (layer, 0, 0, j)),
                  pl.BlockSpec((None, 4, bw, tn), lambda j, i: (layer, 0, 0, j))],
        out_specs=pl.BlockSpec((tm, tn), lambda j, i: (i, j)),
        compiler_params=_params(2),
        name="gated_merge",
    )(h, *ys, wg, wb)


def _residual_ln(x, m, gate, lg, lb, alpha):
    z = alpha * x + (1.0 + gate) * m
    mu = jnp.mean(z, axis=-1, keepdims=True)
    zc = z - mu
    var = jnp.mean(zc * zc, axis=-1, keepdims=True)
    return zc * lax.rsqrt(var + LN_EPS) * lg + lb


def _ln_mod_kernel(x_ref, m_ref, gate_ref, lg_ref, lb_ref, sc_ref, sh_ref, xo_ref, ho_ref, *, alpha):
    xn = _residual_ln(x_ref[...], m_ref[...], gate_ref[...], lg_ref[...], lb_ref[...], alpha)
    xo_ref[...] = xn
    ho_ref[...] = (xn * (1.0 + sc_ref[...]) + sh_ref[...]).astype(ho_ref.dtype)


def _ln_kernel(x_ref, m_ref, gate_ref, lg_ref, lb_ref, xo_ref, *, alpha):
    xo_ref[...] = _residual_ln(x_ref[...], m_ref[...], gate_ref[...], lg_ref[...], lb_ref[...], alpha)


def _ln_call(x, m, gate, ln_g, ln_b, alpha, sc=None, sh=None):
    b, s, d = x.shape
    ts = _pick(s, (256, 128))
    big = pl.BlockSpec((None, ts, d), lambda i, j: (i, j, 0))
    per_b = pl.BlockSpec((None, 1, d), lambda i, j: (i, 0, 0))
    shared = pl.BlockSpec((1, d), lambda i, j: (0, 0))
    x_shape = jax.ShapeDtypeStruct((b, s, d), F32)
    if sc is None:
        return pl.pallas_call(
            functools.partial(_ln_kernel, alpha=alpha),
            out_shape=x_shape,
            grid=(b, s // ts),
            in_specs=[big, big, per_b, shared, shared],
            out_specs=big,
            compiler_params=_params(2),
            name="residual_ln_last",
        )(x, m, gate, ln_g, ln_b), None
    return pl.pallas_call(
        functools.partial(_ln_mod_kernel, alpha=alpha),
        out_shape=(x_shape, jax.ShapeDtypeStruct((b, s, d), BF16)),
        grid=(b, s // ts),
        in_specs=[big, big, per_b, shared, shared, per_b, per_b],
        out_specs=(big, big),
        compiler_params=_params(2),
        name="residual_ln",
    )(x, m, gate, ln_g, ln_b, sc, sh)


def _ffn_in_kernel(h_ref, wa_ref, wg_ref, o_ref, wabf_ref, wgbf_ref):
    @pl.when(pl.program_id(1) == 0)
    def _():
        wabf_ref[...] = wa_ref[...].astype(BF16)
        wgbf_ref[...] = wg_ref[...].astype(BF16)

    h = h_ref[...]
    a = _dot(h, wabf_ref[...])
    g = _dot(h, wgbf_ref[...])
    o_ref[...] = (_silu(a) * g).astype(o_ref.dtype)


def _ffn_in_call(h, w, layer):
    m, d = h.shape
    f = w.shape[2] // 2
    tm = _pick(m, (1024, 512, 256, 128))
    tn = _pick(f, (256, 128))
    nj = f // tn
    return pl.pallas_call(
        _ffn_in_kernel,
        out_shape=jax.ShapeDtypeStruct((m, f), BF16),
        grid=(nj, m // tm),
        in_specs=[pl.BlockSpec((tm, d), lambda j, i: (i, 0)),
                  pl.BlockSpec((None, d, tn), lambda j, i: (layer, 0, j)),
                  pl.BlockSpec((None, d, tn), lambda j, i: (layer, 0, j + nj))],
        out_specs=pl.BlockSpec((tm, tn), lambda j, i: (i, j)),
        scratch_shapes=[pltpu.VMEM((d, tn), BF16), pltpu.VMEM((d, tn), BF16)],
        compiler_params=_params(2),
        name="swiglu_in",
    )(h, w, w)


def _attn_kernel(slope_ref, sink_ref, q_ref, k_ref, v_ref, o_ref, *, seq):
    hkv = pl.program_id(1)
    span = 3 * CHUNK
    scale = HEAD_DIM ** -0.5

    def block(n, carry):
        q0 = pl.multiple_of(n * CHUNK, CHUNK)
        ks = pl.multiple_of(jnp.clip(q0 - CHUNK, 0, seq - span), CHUNK)
        kb = k_ref[pl.ds(ks, span), :].astype(BF16)
        vb = v_ref[pl.ds(ks, span), :].astype(BF16)
        qpos = q0 + lax.broadcasted_iota(jnp.int32, (CHUNK, span), 0)
        kpos = ks + lax.broadcasted_iota(jnp.int32, (CHUNK, span), 1)
        dist = jnp.abs(qpos - kpos)
        valid = dist <= ATT_WINDOW
        distf = dist.astype(F32)
        for g in range(ATT_GROUP):
            head = hkv * ATT_GROUP + g
            qg = q_ref[pl.ds(q0, CHUNK), g * HEAD_DIM:(g + 1) * HEAD_DIM].astype(BF16)
            sc = _dot_nt(qg, kb) * scale - slope_ref[head] * distf
            sc = jnp.where(valid, sc, -jnp.inf)
            sink = sink_ref[head]
            mx = jnp.maximum(jnp.max(sc, axis=-1, keepdims=True), sink)
            p = jnp.exp(sc - mx)
            den = jnp.sum(p, axis=-1, keepdims=True) + jnp.exp(sink - mx)
            o = _dot(p.astype(BF16), vb) / den
            o_ref[pl.ds(q0, CHUNK), g * HEAD_DIM:(g + 1) * HEAD_DIM] = o.astype(o_ref.dtype)
        return carry

    lax.fori_loop(0, seq // CHUNK, block, 0)


def _attn_call(proj, slopes, sink):
    b, s, _ = proj.shape
    qw = ATT_GROUP * HEAD_DIM
    smem = pl.BlockSpec(memory_space=pltpu.SMEM)
    return pl.pallas_call(
        functools.partial(_attn_kernel, seq=s),
        out_shape=jax.ShapeDtypeStruct((b, s, BRANCH_W), BF16),
        grid=(b, ATT_KV_HEADS),
        in_specs=[smem, smem,
                  pl.BlockSpec((None, s, qw), lambda i, h: (i, 0, h)),
                  pl.BlockSpec((None, s, HEAD_DIM), lambda i, h: (i, 0, _OFF_AK + h)),
                  pl.BlockSpec((None, s, HEAD_DIM), lambda i, h: (i, 0, _OFF_AV + h))],
        out_specs=pl.BlockSpec((None, s, qw), lambda i, h: (i, 0, h)),
        compiler_params=_params(2),
        name="windowed_gqa",
    )(slopes, sink, proj, proj, proj)


def _ret_kernel(lg_ref, q_ref, k_ref, v_ref, g_ref, o_ref, sf_ref, *, seq):
    head = pl.program_id(1)
    lg = lg_ref[head]
    nc = seq // CHUNK
    c = CHUNK
    row = lax.broadcasted_iota(jnp.int32, (c, c), 0)
    col = lax.broadcasted_iota(jnp.int32, (c, c), 1)
    dmat = jnp.exp(lg * jnp.abs(row - col).astype(F32))
    pos = lax.broadcasted_iota(jnp.int32, (c, 1), 0).astype(F32)
    kdec_f = jnp.exp(lg * (c - 1.0 - pos))
    kdec_b = jnp.exp(lg * pos)
    qdec_f = jnp.exp(lg * (pos + 1.0))
    qdec_b = jnp.exp(lg * (c - pos))
    cdec = jnp.exp(lg * c)
    kscale = HEAD_DIM ** -0.5

    def fwd(n, st):
        r0 = pl.multiple_of(n * c, c)
        sf_ref[n] = st
        kc = k_ref[pl.ds(r0, c), :] * kscale
        vc = v_ref[pl.ds(r0, c), :].astype(BF16)
        return st * cdec + _dot_tn((kc * kdec_f).astype(BF16), vc)

    lax.fori_loop(0, nc, fwd, jnp.zeros((c, c), F32))

    def bwd(i, st):
        n = nc - 1 - i
        r0 = pl.multiple_of(n * c, c)
        qc = q_ref[pl.ds(r0, c), :]
        kc = k_ref[pl.ds(r0, c), :] * kscale
        vc = v_ref[pl.ds(r0, c), :].astype(BF16)
        p = _dot_nt(qc.astype(BF16), kc.astype(BF16)) * dmat
        o = _dot(p.astype(BF16), vc)
        o = o + _dot((qc * qdec_f).astype(BF16), sf_ref[n].astype(BF16))
        o = o + _dot((qc * qdec_b).astype(BF16), st.astype(BF16))
        mu = jnp.mean(o, axis=-1, keepdims=True)
        oc = o - mu
        var = jnp.mean(oc * oc, axis=-1, keepdims=True)
        y = oc * lax.rsqrt(var + LN_EPS) * _silu(g_ref[pl.ds(r0, c), :])
        o_ref[pl.ds(r0, c), :] = y.astype(o_ref.dtype)
        return st * cdec + _dot_tn((kc * kdec_b).astype(BF16), vc)

    lax.fori_loop(0, nc, bwd, jnp.zeros((c, c), F32))


def _ret_call(proj, log_gamma):
    b, s, _ = proj.shape

    def col(off):
        return pl.BlockSpec((None, s, HEAD_DIM), lambda i, h: (i, 0, off + h))

    return pl.pallas_call(
        functools.partial(_ret_kernel, seq=s),
        out_shape=jax.ShapeDtypeStruct((b, s, BRANCH_W), BF16),
        grid=(b, RET_HEADS),
        in_specs=[pl.BlockSpec(memory_space=pltpu.SMEM), col(_OFF_RQ), col(_OFF_RK), col(_OFF_RV), col(_OFF_RG)],
        out_specs=pl.BlockSpec((None, s, HEAD_DIM), lambda i, h: (i, 0, h)),
        scratch_shapes=[pltpu.VMEM((s // CHUNK, CHUNK, CHUNK), F32)],
        compiler_params=_params(2),
        name="retention",
    )(log_gamma, proj, proj, proj, proj)


def _sgu_kernel(u0_ref, u1_ref, v0_ref, v1_ref, w_ref, b_ref, o_ref):
    half = u0_ref.shape[-1]
    v = jnp.concatenate([_gelu_tanh(v0_ref[...]), _gelu_tanh(v1_ref[...])], axis=-1)
    mu = jnp.mean(v, axis=-1, keepdims=True)
    vc = v - mu
    var = jnp.mean(vc * vc, axis=-1, keepdims=True)
    vn = (vc * lax.rsqrt(var + LN_EPS)).astype(BF16)
    for g in range(SG_GROUPS):
        lo = g * HEAD_DIM
        mixed = _dot(w_ref[g], vn[:, lo:lo + HEAD_DIM]) + b_ref[:, g:g + 1]
        u_ref = u0_ref if lo < half else u1_ref
        ul = lo % half
        u = _gelu_tanh(u_ref[:, ul:ul + HEAD_DIM])
        o_ref[:, lo:lo + HEAD_DIM] = (u * mixed).astype(o_ref.dtype)


def _sgu_call(proj, sg_w, sg_bt):
    b, s, _ = proj.shape
    half = SG_GROUPS * HEAD_DIM // 2
    hb = half // HEAD_DIM

    def blk(off):
        return pl.BlockSpec((None, CHUNK, half), lambda i, n: (i, n, off))

    return pl.pallas_call(
        _sgu_kernel,
        out_shape=jax.ShapeDtypeStruct((b, s, BRANCH_W), BF16),
        grid=(b, s // CHUNK),
        in_specs=[blk(_OFF_SU // hb), blk(_OFF_SU // hb + 1), blk(_OFF_SV // hb), blk(_OFF_SV // hb + 1),
                  pl.BlockSpec((SG_GROUPS, CHUNK, CHUNK), lambda i, n: (0, 0, 0)),
                  pl.BlockSpec((CHUNK, SG_GROUPS), lambda i, n: (0, 0))],
        out_specs=pl.BlockSpec((None, CHUNK, BRANCH_W), lambda i, n: (i, n, 0)),
        compiler_params=_params(2),
        name="spatial_gating",
    )(proj, proj, proj, proj, sg_w, sg_bt)


def _split3(x):
    x1 = x.astype(BF16)
    r1 = x - x1.astype(F32)
    x2 = r1.astype(BF16)
    x3 = (r1 - x2.astype(F32)).astype(BF16)
    return x1, x2, x3


def _hgrn_chunk(q, z, v, lb, st, forward):
    c = CHUNK
    f = lb + (1.0 - lb) * _sigmoid(z)
    kk = 1.0 - f
    logf = jnp.log(f)
    row = lax.broadcasted_iota(jnp.int32, (c, c), 0)
    col = lax.broadcasted_iota(jnp.int32, (c, c), 1)
    allowed = (col <= row) if forward else (col >= row)

    lmat = jnp.where(allowed, 1.0, 0.0)
    mats = [lmat]
    levels = []
    b = c // 2
    while b >= SUB:
        blk = (row // (2 * b)) * (2 * b)
        mid = blk + (b - 1 if forward else b)
        mmat = jnp.where((col <= mid) if forward else (col >= mid), 1.0, 0.0)
        mats.append(lmat - mmat)
        levels.append(b)
        b //= 2
    stacked = jnp.concatenate(mats, axis=0).astype(BF16)
    g1, g2, g3 = _split3(logf)
    sums = _dot(stacked, g1) + _dot(stacked, g2) + _dot(stacked, g3)
    a = sums[0:c]
    a_end = a[c - 1:c] if forward else a[0:1]

    vb = v.astype(BF16)
    o = _dot_nt((q * jnp.exp(a)).astype(BF16), st.astype(BF16))
    st_new = st * jnp.exp(a_end) + _dot_tn(vb, (kk * jnp.exp(a_end - a)).astype(BF16))

    rowv = lax.broadcasted_iota(jnp.int32, (c, 1), 0)
    p = jnp.zeros((c, c), F32)
    for li, b in enumerate(levels):
        d = sums[(li + 1) * c:(li + 2) * c]
        upper = (rowv % (2 * b)) >= b
        q_side = upper if forward else jnp.logical_not(upper)
        e = jnp.exp(jnp.where(q_side, d, -d))
        qt = jnp.where(q_side, q * e, 0.0).astype(BF16)
        kt = jnp.where(q_side, 0.0, kk * e).astype(BF16)
        pl_ = _dot_nt(qt, kt)
        if 2 * b < c:
            pl_ = jnp.where((row // (2 * b)) == (col // (2 * b)), pl_, 0.0)
        p = p + pl_

    rsub = rowv % SUB
    for delta in range(SUB):
        if delta == 0:
            w = q * kk
        else:
            sh = delta if forward else c - delta
            a_s = pltpu.roll(a, sh, 0)
            k_s = pltpu.roll(kk, sh, 0)
            w = q * k_s * jnp.exp(jnp.minimum(a - a_s, 0.0))
        rs = jnp.sum(w, axis=-1, keepdims=True)
        if forward:
            hit = (col == row - delta) & (rsub >= delta)
        else:
            hit = (col == row + delta) & (rsub + delta < SUB)
        p = jnp.where(hit, rs, p)

    o = o + _dot(p.astype(BF16), vb)
    return o, st_new


def _hgrn_kernel(q_ref, zf_ref, zb_ref, v_ref, g_ref, lb_ref, o_ref, of_ref, *, seq):
    nc = seq // CHUNK
    c = CHUNK
    lb = lb_ref[...]

    def fwd(n, st):
        r0 = pl.multiple_of(n * c, c)
        o, st = _hgrn_chunk(q_ref[pl.ds(r0, c), :], zf_ref[pl.ds(r0, c), :], v_ref[pl.ds(r0, c), :], lb, st, True)
        of_ref[pl.ds(r0, c), :] = o
        return st

    lax.fori_loop(0, nc, fwd, jnp.zeros((c, c), F32))

    def bwd(i, st):
        n = nc - 1 - i
        r0 = pl.multiple_of(n * c, c)
        o, st = _hgrn_chunk(q_ref[pl.ds(r0, c), :], zb_ref[pl.ds(r0, c), :], v_ref[pl.ds(r0, c), :], lb, st, False)
        o = o + of_ref[pl.ds(r0, c), :]
        y = o * lax.rsqrt(jnp.mean(o * o, axis=-1, keepdims=True) + LN_EPS) * _silu(g_ref[pl.ds(r0, c), :])
        o_ref[pl.ds(r0, c), :] = y.astype(o_ref.dtype)
        return st

    lax.fori_loop(0, nc, bwd, jnp.zeros((c, c), F32))


def _hgrn_call(proj, lb):
    b, s, _ = proj.shape

    def col(off):
        return pl.BlockSpec((None, s, HEAD_DIM), lambda i, h: (i, 0, off + h))

    return pl.pallas_call(
        functools.partial(_hgrn_kernel, seq=s),
        out_shape=jax.ShapeDtypeStruct((b, s, BRANCH_W), BF16),
        grid=(b, HGRN_HEADS),
        in_specs=[col(_OFF_DQ), col(_OFF_DFF), col(_OFF_DFB), col(_OFF_DI), col(_OFF_DG),
                  pl.BlockSpec((1, HEAD_DIM), lambda i, h: (0, h))],
        out_specs=pl.BlockSpec((None, s, HEAD_DIM), lambda i, h: (i, 0, h)),
        scratch_shapes=[pltpu.VMEM((s, HEAD_DIM), F32)],
        compiler_params=_params(2),
        name="hgrn2",
    )(proj, proj, proj, proj, proj, lb)


def kernel(x, c, w_in, attn_sink, sg_w, sg_b, hgrn_lb_logits, w_branch, w_gate, w_o, w_mod, b_mod, ln_g, ln_b, w_ffn_in, w_ffn_out):
    bsz, seq, d = x.shape
    depth = w_in.shape[0]
    m = bsz * seq
    alpha = (2.0 * depth) ** 0.25

    pz = jax.nn.softmax(hgrn_lb_logits.astype(F32), axis=0)
    lower_bounds = jnp.cumsum(pz, axis=0) - pz[:1]
    slopes = jnp.exp2(-8.0 * jnp.arange(1, ATT_HEADS + 1, dtype=F32) / ATT_HEADS)
    log_gamma = jnp.log1p(-jnp.exp2(-5.0 - jnp.arange(RET_HEADS, dtype=F32)))

    rows = 8
    c_pad = jnp.zeros((rows, d), F32).at[:bsz].set(c)
    mod = _mod_call(c_pad, w_mod, b_mod)[:, :bsz]

    def mod_part(l, i):
        return mod[l, :, i * d:(i + 1) * d].reshape(bsz, 1, d)

    tm = _pick(m, (1024, 512, 256, 128))
    tn = _pick(d, (512, 256, 128))
    wg_bf, wb_bf, wfo_bf = w_gate.astype(BF16), w_branch.astype(BF16), w_ffn_out.astype(BF16)
    h = _modulate_call(x, mod_part(0, 1), mod_part(0, 0))
    for l in range(depth):
        h2d = h.reshape(m, d)
        proj = _matmul_ws(h2d, w_in, l, tm=tm, tn=_pick(w_in.shape[2], (512, 256, 128)),
                          out_dtype=F32, name="in_proj").reshape(bsz, seq, -1)
        y_a = _attn_call(proj, slopes, attn_sink[l].astype(F32))
        y_b = _ret_call(proj, log_gamma)
        y_c = _sgu_call(proj, sg_w[l].astype(BF16), sg_b[l].T)
        y_d = _hgrn_call(proj, lower_bounds[l].reshape(1, -1))
        ys = [y.reshape(m, BRANCH_W) for y in (y_a, y_b, y_c, y_d)]
        merged = _merge_call(h2d, ys, wg_bf, wb_bf, l)
        mix = _matmul_ws(merged, w_o, l, tm=tm, tn=tn, out_dtype=F32, name="out_proj").reshape(bsz, seq, d)
        x, h = _ln_call(x, mix, mod_part(l, 2), ln_g[l, 0:1], ln_b[l, 0:1], alpha, mod_part(l, 4), mod_part(l, 3))
        act = _ffn_in_call(h.reshape(m, d), w_ffn_in, l)
        ffn = _matmul(act, wfo_bf, l, tm=_pick(m, (512, 256, 128)), tn=tn,
                      out_dtype=F32, name="ffn_out").reshape(bsz, seq, d)
        if l + 1 < depth:
            x, h = _ln_call(x, ffn, mod_part(l, 5), ln_g[l, 1:2], ln_b[l, 1:2], alpha,
                            mod_part(l + 1, 1), mod_part(l + 1, 0))
        else:
            x, _ = _ln_call(x, ffn, mod_part(l, 5), ln_g[l, 1:2], ln_b[l, 1:2], alpha)
    return x
```

```python
import functools

import jax
import jax.numpy as jnp
import numpy as np
from jax import lax
from jax.experimental import pallas as pl
from jax.experimental.pallas import tpu as pltpu

F32 = jnp.float32
BF16 = jnp.bfloat16

HEAD_DIM = 128
ATT_HEADS = 8
ATT_KV_HEADS = 2
ATT_GROUP = ATT_HEADS // ATT_KV_HEADS
ATT_WINDOW = 128
RET_HEADS = 8
SG_GROUPS = 8
HGRN_HEADS = 8
CHUNK = 128
SUB = 4
_HGRN_LEVELS = (CHUNK // SUB).bit_length() - 1
HGRN_UNROLL = 4
RET_UNROLL = 4
ATT_UNROLL = 2
BRANCH_W = 8 * HEAD_DIM
LN_EPS = 1e-5

_OFF_AQ = 0
_OFF_AK = _OFF_AQ + ATT_HEADS
_OFF_AV = _OFF_AK + ATT_KV_HEADS
_OFF_RQ = _OFF_AV + ATT_KV_HEADS
_OFF_RK = _OFF_RQ + RET_HEADS
_OFF_RV = _OFF_RK + RET_HEADS
_OFF_RG = _OFF_RV + RET_HEADS
_OFF_SU = _OFF_RG + RET_HEADS
_OFF_SV = _OFF_SU + SG_GROUPS
_OFF_DQ = _OFF_SV + SG_GROUPS
_OFF_DFF = _OFF_DQ + HGRN_HEADS
_OFF_DFB = _OFF_DFF + HGRN_HEADS
_OFF_DI = _OFF_DFB + HGRN_HEADS
_OFF_DG = _OFF_DI + HGRN_HEADS

VMEM_LIMIT_BYTES_V7X = 56 * 1024 * 1024


def _params(n_axes):
    return pltpu.CompilerParams(dimension_semantics=("arbitrary",) * n_axes,
                                vmem_limit_bytes=VMEM_LIMIT_BYTES_V7X)


def _dot(a, b):
    return jnp.dot(a, b, preferred_element_type=F32)


def _dot_nt(a, b):
    return lax.dot_general(a, b, (((1,), (1,)), ((), ())), preferred_element_type=F32)


def _dot_tn(a, b):
    return lax.dot_general(a, b, (((0,), (0,)), ((), ())), preferred_element_type=F32)


def _sigmoid(x):
    return 1.0 / (1.0 + jnp.exp(-x))


def _silu(x):
    return x * _sigmoid(x)


def _gelu_tanh(x):
    return 0.5 * x * (1.0 + jnp.tanh(0.7978845608028654 * (x + 0.044715 * (x * x * x))))


def _pick(n, prefs):
    for p in prefs:
        if n % p == 0:
            return p
    return n


def _mod_kernel(c_ref, w_ref, b_ref, o_ref):
    ca = _silu(c_ref[...]).astype(BF16)
    o_ref[...] = _dot(ca, w_ref[...].astype(BF16)) + b_ref[...]


def _mod_call(c_pad, w_mod, b_mod):
    depth, d, n = w_mod.shape
    rows = c_pad.shape[0]
    tn = _pick(n, (512, 256, 128))
    return pl.pallas_call(
        _mod_kernel,
        out_shape=jax.ShapeDtypeStruct((depth, rows, n), F32),
        grid=(depth, n // tn),
        in_specs=[pl.BlockSpec((rows, d), lambda l, j: (0, 0)),
                  pl.BlockSpec((None, d, tn), lambda l, j: (l, 0, j)),
                  pl.BlockSpec((None, 1, tn), lambda l, j: (l, 0, j))],
        out_specs=pl.BlockSpec((None, rows, tn), lambda l, j: (l, 0, j)),
        compiler_params=_params(2),
        name="adaln_mod",
    )(c_pad, w_mod, b_mod.reshape(depth, 1, n))


def _modulate_kernel(x_ref, sc_ref, sh_ref, o_ref):
    o_ref[...] = (x_ref[...] * (1.0 + sc_ref[...]) + sh_ref[...]).astype(o_ref.dtype)


def _modulate_call(x, sc, sh):
    b, s, d = x.shape
    ts = _pick(s, (512, 256, 128))
    return pl.pallas_call(
        _modulate_kernel,
        out_shape=jax.ShapeDtypeStruct((b, s, d), BF16),
        grid=(b, s // ts),
        in_specs=[pl.BlockSpec((None, ts, d), lambda i, j: (i, j, 0)),
                  pl.BlockSpec((None, 1, d), lambda i, j: (i, 0, 0)),
                  pl.BlockSpec((None, 1, d), lambda i, j: (i, 0, 0))],
        out_specs=pl.BlockSpec((None, ts, d), lambda i, j: (i, j, 0)),
        compiler_params=_params(2),
        name="modulate",
    )(x, sc, sh)


def _mm_kernel(x_ref, w_ref, o_ref):
    o_ref[...] = _dot(x_ref[...], w_ref[...].astype(BF16)).astype(o_ref.dtype)


def _matmul(x, w, layer, *, tm, tn, out_dtype, name):
    m, k = x.shape
    n = w.shape[2]
    return pl.pallas_call(
        _mm_kernel,
        out_shape=jax.ShapeDtypeStruct((m, n), out_dtype),
        grid=(m // tm, n // tn),
        in_specs=[pl.BlockSpec((tm, k), lambda i, j: (i, 0)),
                  pl.BlockSpec((None, k, tn), lambda i, j: (layer, 0, j))],
        out_specs=pl.BlockSpec((tm, tn), lambda i, j: (i, j)),
        compiler_params=_params(2),
        name=name,
    )(x, w)


def _merge_kernel(h_ref, ya_ref, yb_ref, yc_ref, yd_ref, wg_ref, wb_ref, o_ref):
    h = h_ref[...]
    acc = None
    for i, y_ref in enumerate((ya_ref, yb_ref, yc_ref, yd_ref)):
        gate = _sigmoid(_dot(h, wg_ref[i]))
        term = gate * _dot(y_ref[...], wb_ref[i])
        acc = term if acc is None else acc + term
    o_ref[...] = acc.astype(o_ref.dtype)


def _merge_call(h, ys, wg, wb, layer):
    m, d = h.shape
    bw = ys[0].shape[1]
    n = wg.shape[3]
    tm = _pick(m, (512, 256, 128))
    tn = _pick(n, (256, 128))
    y_spec = pl.BlockSpec((tm, bw), lambda j, i: (i, 0))
    return pl.pallas_call(
        _merge_kernel,
        out_shape=jax.ShapeDtypeStruct((m, n), BF16),
        grid=(n // tn, m // tm),
        in_specs=[pl.BlockSpec((tm, d), lambda j, i: (i, 0)), y_spec, y_spec, y_spec, y_spec,
                  pl.BlockSpec((None, 4, d, tn), lambda j, i: (layer, 0, 0, j)),
                  pl.BlockSpec((None, 4, bw, tn), lambda j, i: (layer, 0, 0, j))],
        out_specs=pl.BlockSpec((tm, tn), lambda j, i: (i, j)),
        compiler_params=_params(2),
        name="gated_merge",
    )(h, *ys, wg, wb)


def _residual_ln(x, m, gate, lg, lb, alpha):
    z = alpha * x + (1.0 + gate) * m
    mu = jnp.mean(z, axis=-1, keepdims=True)
    zc = z - mu
    var = jnp.mean(zc * zc, axis=-1, keepdims=True)
    return zc * lax.rsqrt(var + LN_EPS) * lg + lb


def _ln_mod_kernel(x_ref, m_ref, gate_ref, lg_ref, lb_ref, sc_ref, sh_ref, xo_ref, ho_ref, *, alpha):
    xn = _residual_ln(x_ref[...], m_ref[...], gate_ref[...], lg_ref[...], lb_ref[...], alpha)
    xo_ref[...] = xn
    ho_ref[...] = (xn * (1.0 + sc_ref[...]) + sh_ref[...]).astype(ho_ref.dtype)


def _ln_kernel(x_ref, m_ref, gate_ref, lg_ref, lb_ref, xo_ref, *, alpha):
    xo_ref[...] = _residual_ln(x_ref[...], m_ref[...], gate_ref[...], lg_ref[...], lb_ref[...], alpha)


def _ln_call(x, m, gate, ln_g, ln_b, alpha, sc=None, sh=None):
    b, s, d = x.shape
    ts = _pick(s, (256, 128))
    big = pl.BlockSpec((None, ts, d), lambda i, j: (i, j, 0))
    per_b = pl.BlockSpec((None, 1, d), lambda i, j: (i, 0, 0))
    shared = pl.BlockSpec((1, d), lambda i, j: (0, 0))
    x_shape = jax.ShapeDtypeStruct((b, s, d), F32)
    if sc is None:
        return pl.pallas_call(
            functools.partial(_ln_kernel, alpha=alpha),
            out_shape=x_shape,
            grid=(b, s // ts),
            in_specs=[big, big, per_b, shared, shared],
            out_specs=big,
            compiler_params=_params(2),
            name="residual_ln_last",
        )(x, m, gate, ln_g, ln_b), None
    return pl.pallas_call(
        functools.partial(_ln_mod_kernel, alpha=alpha),
        out_shape=(x_shape, jax.ShapeDtypeStruct((b, s, d), BF16)),
        grid=(b, s // ts),
        in_specs=[big, big, per_b, shared, shared, per_b, per_b],
        out_specs=(big, big),
        compiler_params=_params(2),
        name="residual_ln",
    )(x, m, gate, ln_g, ln_b, sc, sh)


def _ffn_in_kernel(h_ref, wa_ref, wg_ref, o_ref):
    h = h_ref[...]
    a = _dot(h, wa_ref[...].astype(BF16))
    g = _dot(h, wg_ref[...].astype(BF16))
    o_ref[...] = (_silu(a) * g).astype(o_ref.dtype)


def _ffn_in_call(h, w, layer, tm):
    m, d = h.shape
    f = w.shape[2] // 2
    tn = _pick(f, (256, 128))
    nj = f // tn
    return pl.pallas_call(
        _ffn_in_kernel,
        out_shape=jax.ShapeDtypeStruct((m, f), BF16),
        grid=(m // tm, nj),
        in_specs=[pl.BlockSpec((tm, d), lambda i, j: (i, 0)),
                  pl.BlockSpec((None, d, tn), lambda i, j: (layer, 0, j)),
                  pl.BlockSpec((None, d, tn), lambda i, j: (layer, 0, j + nj))],
        out_specs=pl.BlockSpec((tm, tn), lambda i, j: (i, j)),
        compiler_params=_params(2),
        name="swiglu_in",
    )(h, w, w)


def _attn_kernel(slope_ref, sink_ref, q_ref, k_ref, v_ref, o_ref, *, seq):
    hkv = pl.program_id(1)
    span = 3 * CHUNK
    scale = HEAD_DIM ** -0.5

    def block(n, carry):
        q0 = pl.multiple_of(n * CHUNK, CHUNK)
        ks = pl.multiple_of(jnp.clip(q0 - CHUNK, 0, seq - span), CHUNK)
        kb = k_ref[pl.ds(ks, span), :].astype(BF16)
        vb = v_ref[pl.ds(ks, span), :].astype(BF16)
        qpos = q0 + lax.broadcasted_iota(jnp.int32, (CHUNK, span), 0)
        kpos = ks + lax.broadcasted_iota(jnp.int32, (CHUNK, span), 1)
        dist = jnp.abs(qpos - kpos)
        valid = dist <= ATT_WINDOW
        distf = dist.astype(F32)
        for g in range(ATT_GROUP):
            head = hkv * ATT_GROUP + g
            qg = q_ref[pl.ds(q0, CHUNK), g * HEAD_DIM:(g + 1) * HEAD_DIM].astype(BF16)
            sc = _dot_nt(qg, kb) * scale - slope_ref[head] * distf
            sc = jnp.where(valid, sc, -jnp.inf)
            sink = sink_ref[head]
            mx = jnp.maximum(jnp.max(sc, axis=-1, keepdims=True), sink)
            p = jnp.exp(sc - mx)
            den = jnp.sum(p, axis=-1, keepdims=True) + jnp.exp(sink - mx)
            o = _dot(p.astype(BF16), vb) / den
            o_ref[pl.ds(q0, CHUNK), g * HEAD_DIM:(g + 1) * HEAD_DIM] = o.astype(o_ref.dtype)
        return carry

    lax.fori_loop(0, seq // CHUNK, block, 0, unroll=ATT_UNROLL)


def _attn_call(proj, slopes, sink):
    b, s, _ = proj.shape
    qw = ATT_GROUP * HEAD_DIM
    smem = pl.BlockSpec(memory_space=pltpu.SMEM)
    return pl.pallas_call(
        functools.partial(_attn_kernel, seq=s),
        out_shape=jax.ShapeDtypeStruct((b, s, BRANCH_W), BF16),
        grid=(b, ATT_KV_HEADS),
        in_specs=[smem, smem,
                  pl.BlockSpec((None, s, qw), lambda i, h: (i, 0, h)),
                  pl.BlockSpec((None, s, HEAD_DIM), lambda i, h: (i, 0, _OFF_AK + h)),
                  pl.BlockSpec((None, s, HEAD_DIM), lambda i, h: (i, 0, _OFF_AV + h))],
        out_specs=pl.BlockSpec((None, s, qw), lambda i, h: (i, 0, h)),
        compiler_params=_params(2),
        name="windowed_gqa",
    )(slopes, sink, proj, proj, proj)


def _ret_kernel(lg_ref, q_ref, k_ref, v_ref, g_ref, o_ref, sf_ref, *, seq):
    head = pl.program_id(1)
    lg = lg_ref[head]
    nc = seq // CHUNK
    c = CHUNK
    row = lax.broadcasted_iota(jnp.int32, (c, c), 0)
    col = lax.broadcasted_iota(jnp.int32, (c, c), 1)
    dmat = jnp.exp(lg * jnp.abs(row - col).astype(F32))
    pos = lax.broadcasted_iota(jnp.int32, (c, 1), 0).astype(F32)
    kdec_f = jnp.exp(lg * (c - 1.0 - pos))
    kdec_b = jnp.exp(lg * pos)
    qdec_f = jnp.exp(lg * (pos + 1.0))
    qdec_b = jnp.exp(lg * (c - pos))
    cdec = jnp.exp(lg * c)
    kscale = HEAD_DIM ** -0.5

    def fwd(n, st):
        r0 = pl.multiple_of(n * c, c)
        sf_ref[n] = st
        kc = k_ref[pl.ds(r0, c), :] * kscale
        vc = v_ref[pl.ds(r0, c), :].astype(BF16)
        return st * cdec + _dot_tn((kc * kdec_f).astype(BF16), vc)

    lax.fori_loop(0, nc, fwd, jnp.zeros((c, c), F32), unroll=RET_UNROLL)

    def bwd(i, st):
        n = nc - 1 - i
        r0 = pl.multiple_of(n * c, c)
        qc = q_ref[pl.ds(r0, c), :]
        kc = k_ref[pl.ds(r0, c), :] * kscale
        vc = v_ref[pl.ds(r0, c), :].astype(BF16)
        p = _dot_nt(qc.astype(BF16), kc.astype(BF16)) * dmat
        o = _dot(p.astype(BF16), vc)
        o = o + _dot((qc * qdec_f).astype(BF16), sf_ref[n].astype(BF16))
        o = o + _dot((qc * qdec_b).astype(BF16), st.astype(BF16))
        mu = jnp.mean(o, axis=-1, keepdims=True)
        oc = o - mu
        var = jnp.mean(oc * oc, axis=-1, keepdims=True)
        y = oc * lax.rsqrt(var + LN_EPS) * _silu(g_ref[pl.ds(r0, c), :])
        o_ref[pl.ds(r0, c), :] = y.astype(o_ref.dtype)
        return st * cdec + _dot_tn((kc * kdec_b).astype(BF16), vc)

    lax.fori_loop(0, nc, bwd, jnp.zeros((c, c), F32), unroll=RET_UNROLL)


def _ret_call(proj, log_gamma):
    b, s, _ = proj.shape

    def col(off):
        return pl.BlockSpec((None, s, HEAD_DIM), lambda i, h: (i, 0, off + h))

    return pl.pallas_call(
        functools.partial(_ret_kernel, seq=s),
        out_shape=jax.ShapeDtypeStruct((b, s, BRANCH_W), BF16),
        grid=(b, RET_HEADS),
        in_specs=[pl.BlockSpec(memory_space=pltpu.SMEM), col(_OFF_RQ), col(_OFF_RK), col(_OFF_RV), col(_OFF_RG)],
        out_specs=pl.BlockSpec((None, s, HEAD_DIM), lambda i, h: (i, 0, h)),
        scratch_shapes=[pltpu.VMEM((s // CHUNK, CHUNK, CHUNK), F32)],
        compiler_params=_params(2),
        name="retention",
    )(log_gamma, proj, proj, proj, proj)


def _sgu_kernel(u0_ref, u1_ref, v0_ref, v1_ref, w_ref, b_ref, o_ref):
    half = u0_ref.shape[-1]
    v = jnp.concatenate([_gelu_tanh(v0_ref[...]), _gelu_tanh(v1_ref[...])], axis=-1)
    mu = jnp.mean(v, axis=-1, keepdims=True)
    vc = v - mu
    var = jnp.mean(vc * vc, axis=-1, keepdims=True)
    vn = (vc * lax.rsqrt(var + LN_EPS)).astype(BF16)
    for g in range(SG_GROUPS):
        lo = g * HEAD_DIM
        mixed = _dot(w_ref[g], vn[:, lo:lo + HEAD_DIM]) + b_ref[:, g:g + 1]
        u_ref = u0_ref if lo < half else u1_ref
        ul = lo % half
        u = _gelu_tanh(u_ref[:, ul:ul + HEAD_DIM])
        o_ref[:, lo:lo + HEAD_DIM] = (u * mixed).astype(o_ref.dtype)


def _sgu_call(proj, sg_w, sg_bt):
    b, s, _ = proj.shape
    half = SG_GROUPS * HEAD_DIM // 2
    hb = half // HEAD_DIM

    def blk(off):
        return pl.BlockSpec((None, CHUNK, half), lambda i, n: (i, n, off))

    return pl.pallas_call(
        _sgu_kernel,
        out_shape=jax.ShapeDtypeStruct((b, s, BRANCH_W), BF16),
        grid=(b, s // CHUNK),
        in_specs=[blk(_OFF_SU // hb), blk(_OFF_SU // hb + 1), blk(_OFF_SV // hb), blk(_OFF_SV // hb + 1),
                  pl.BlockSpec((SG_GROUPS, CHUNK, CHUNK), lambda i, n: (0, 0, 0)),
                  pl.BlockSpec((CHUNK, SG_GROUPS), lambda i, n: (0, 0))],
        out_specs=pl.BlockSpec((None, CHUNK, BRANCH_W), lambda i, n: (i, n, 0)),
        compiler_params=_params(2),
        name="spatial_gating",
    )(proj, proj, proj, proj, sg_w, sg_bt)


def _split3(x):
    x1 = x.astype(BF16)
    r1 = x - x1.astype(F32)
    x2 = r1.astype(BF16)
    x3 = (r1 - x2.astype(F32)).astype(BF16)
    return x1, x2, x3


def _hgrn_tables(forward):
    c = CHUNK
    row = np.arange(c)[:, None]
    col = np.arange(c)[None, :]
    lmat = ((col <= row) if forward else (col >= row)).astype(np.float32)
    qms, lms = [], []
    b = c // 2
    while b >= SUB:
        upper_r = (row % (2 * b)) >= b
        upper_c = (col % (2 * b)) >= b
        q_side = upper_r if forward else ~upper_r
        k_side = ~upper_c if forward else upper_c
        qms.append(np.broadcast_to(q_side, (c, HEAD_DIM)).astype(np.float32))
        lms.append((q_side & k_side & ((row // (2 * b)) == (col // (2 * b)))).astype(np.float32))
        b //= 2
    off = np.where((row // SUB) == (col // SUB), col - row, 2 * c).astype(np.int32)
    return lmat, np.stack(qms), np.stack(lms), off


def _hgrn_chunk(q, z, v, lb, st, tmat, qmask_ref, lmask_ref, off3, forward):
    c = CHUNK
    f = lb + (1.0 - lb) * _sigmoid(z)
    kk = 1.0 - f
    g1, g2, g3 = _split3(jnp.log2(f))
    a = _dot(tmat, g1) + _dot(tmat, g2) + _dot(tmat, g3)
    a_end = a[c - 1:c] if forward else a[0:1]

    vb = v.astype(BF16)
    o = _dot_nt((q * jnp.exp2(a)).astype(BF16), st.astype(BF16))
    st_new = st * jnp.exp2(a_end) + _dot_tn(vb, (kk * jnp.exp2(a_end - a)).astype(BF16))

    dqk = q - kk
    p = None
    for li in range(_HGRN_LEVELS):
        b = c >> (li + 1)
        a4 = a.reshape(c // (2 * b), 2 * b, c)
        mid = a4[:, b - 1:b, :] if forward else a4[:, b:b + 1, :]
        d = (a4 - mid).reshape(c, c)
        u = ((kk + qmask_ref[li] * dqk) * jnp.exp2(-jnp.abs(d))).astype(BF16)
        pl_ = _dot_nt(u, u) * lmask_ref[li]
        p = pl_ if p is None else p + pl_

    tiles = (c // 8, 8, c)
    a3, k3, q3 = a.reshape(tiles), kk.reshape(tiles), q.reshape(tiles)
    p3 = p.reshape(tiles)
    for delta in range(SUB):
        if delta == 0:
            w = q3 * k3
        else:
            sh = delta if forward else 8 - delta
            a_s = pltpu.roll(a3, sh, 1)
            k_s = pltpu.roll(k3, sh, 1)
            w = q3 * k_s * jnp.exp2(jnp.minimum(a3 - a_s, 0.0))
        rs = jnp.sum(w, axis=-1, keepdims=True)
        p3 = jnp.where(off3 == (-delta if forward else delta), rs, p3)

    o = o + _dot(p3.reshape(c, c).astype(BF16), vb)
    return o, st_new


def _hgrn_kernel(q_ref, zf_ref, zb_ref, v_ref, g_ref, lb_ref, tmat_ref, qmask_ref, lmask_ref, off_ref,
                 o_ref, of_ref, *, seq):
    nc = seq // CHUNK
    c = CHUNK
    lb = lb_ref[...]
    off3 = off_ref[...].reshape(c // 8, 8, c)

    def fwd(n, st):
        r0 = pl.multiple_of(n * c, c)
        o, st = _hgrn_chunk(q_ref[pl.ds(r0, c), :], zf_ref[pl.ds(r0, c), :], v_ref[pl.ds(r0, c), :], lb, st,
                            tmat_ref[0], qmask_ref.at[0], lmask_ref.at[0], off3, True)
        of_ref[pl.ds(r0, c), :] = o
        return st

    lax.fori_loop(0, nc, fwd, jnp.zeros((c, c), F32), unroll=HGRN_UNROLL)

    def bwd(i, st):
        n = nc - 1 - i
        r0 = pl.multiple_of(n * c, c)
        o, st = _hgrn_chunk(q_ref[pl.ds(r0, c), :], zb_ref[pl.ds(r0, c), :], v_ref[pl.ds(r0, c), :], lb, st,
                            tmat_ref[1], qmask_ref.at[1], lmask_ref.at[1], off3, False)
        o = o + of_ref[pl.ds(r0, c), :]
        y = o * lax.rsqrt(jnp.mean(o * o, axis=-1, keepdims=True) + LN_EPS) * _silu(g_ref[pl.ds(r0, c), :])
        o_ref[pl.ds(r0, c), :] = y.astype(o_ref.dtype)
        return st

    lax.fori_loop(0, nc, bwd, jnp.zeros((c, c), F32), unroll=HGRN_UNROLL)


def _hgrn_call(proj, lb):
    b, s, _ = proj.shape
    tf, tb = _hgrn_tables(True), _hgrn_tables(False)
    tmat = jnp.asarray(np.stack([tf[0], tb[0]]), BF16)
    qmask = jnp.asarray(np.stack([tf[1], tb[1]]))
    lmask = jnp.asarray(np.stack([tf[2], tb[2]]))
    off = jnp.asarray(tf[3])

    def col(off_):
        return pl.BlockSpec((None, s, HEAD_DIM), lambda i, h: (i, 0, off_ + h))

    def whole(arr):
        return pl.BlockSpec(arr.shape, lambda i, h: (0,) * arr.ndim)

    return pl.pallas_call(
        functools.partial(_hgrn_kernel, seq=s),
        out_shape=jax.ShapeDtypeStruct((b, s, BRANCH_W), BF16),
        grid=(b, HGRN_HEADS),
        in_specs=[col(_OFF_DQ), col(_OFF_DFF), col(_OFF_DFB), col(_OFF_DI), col(_OFF_DG),
                  pl.BlockSpec((1, HEAD_DIM), lambda i, h: (0, h)),
                  whole(tmat), whole(qmask), whole(lmask), whole(off)],
        out_specs=pl.BlockSpec((None, s, HEAD_DIM), lambda i, h: (i, 0, h)),
        scratch_shapes=[pltpu.VMEM((s, HEAD_DIM), F32)],
        compiler_params=_params(2),
        name="hgrn2",
    )(proj, proj, proj, proj, proj, lb, tmat, qmask, lmask, off)


def kernel(x, c, w_in, attn_sink, sg_w, sg_b, hgrn_lb_logits, w_branch, w_gate, w_o, w_mod, b_mod, ln_g, ln_b, w_ffn_in, w_ffn_out):
    bsz, seq, d = x.shape
    depth = w_in.shape[0]
    m = bsz * seq
    alpha = (2.0 * depth) ** 0.25

    pz = jax.nn.softmax(hgrn_lb_logits.astype(F32), axis=0)
    lower_bounds = jnp.cumsum(pz, axis=0) - pz[:1]
    slopes = jnp.exp2(-8.0 * jnp.arange(1, ATT_HEADS + 1, dtype=F32) / ATT_HEADS)
    log_gamma = jnp.log1p(-jnp.exp2(-5.0 - jnp.arange(RET_HEADS, dtype=F32)))

    rows = 8
    c_pad = jnp.zeros((rows, d), F32).at[:bsz].set(c)
    mod = _mod_call(c_pad, w_mod, b_mod)[:, :bsz]

    def mod_part(l, i):
        return mod[l, :, i * d:(i + 1) * d].reshape(bsz, 1, d)

    tm = _pick(m, (1024, 512, 256, 128))
    tn = _pick(d, (512, 256, 128))
    wg_bf, wb_bf, wfo_bf = w_gate.astype(BF16), w_branch.astype(BF16), w_ffn_out.astype(BF16)
    h = _modulate_call(x, mod_part(0, 1), mod_part(0, 0))
    for l in range(depth):
        h2d = h.reshape(m, d)
        proj = _matmul(h2d, w_in, l, tm=tm, tn=_pick(w_in.shape[2], (512, 256, 128)),
                       out_dtype=F32, name="in_proj").reshape(bsz, seq, -1)
        y_a = _attn_call(proj, slopes, attn_sink[l].astype(F32))
        y_b = _ret_call(proj, log_gamma)
        y_c = _sgu_call(proj, sg_w[l].astype(BF16), sg_b[l].T)
        y_d = _hgrn_call(proj, lower_bounds[l].reshape(1, -1))
        ys = [y.reshape(m, BRANCH_W) for y in (y_a, y_b, y_c, y_d)]
        merged = _merge_call(h2d, ys, wg_bf, wb_bf, l)
        mix = _matmul(merged, w_o, l, tm=tm, tn=tn, out_dtype=F32, name="out_proj").reshape(bsz, seq, d)
        x, h = _ln_call(x, mix, mod_part(l, 2), ln_g[l, 0:1], ln_b[l, 0:1], alpha, mod_part(l, 4), mod_part(l, 3))
        act = _ffn_in_call(h.reshape(m, d), w_ffn_in, l, tm)
        ffn = _matmul(act, wfo_bf, l, tm=_pick(m, (512, 256, 128)), tn=tn,
                      out_dtype=F32, name="ffn_out").reshape(bsz, seq, d)
        if l + 1 < depth:
            x, h = _ln_call(x, ffn, mod_part(l, 5), ln_g[l, 1:2], ln_b[l, 1:2], alpha,
                            mod_part(l + 1, 1), mod_part(l + 1, 0))
        else:
            x, _ = _ln_call(x, ffn, mod_part(l, 5), ln_g[l, 1:2], ln_b[l, 1:2], alpha)
    return x
```

```python
import functools

import jax
import jax.numpy as jnp
import numpy as np
from jax import lax
from jax.experimental import pallas as pl
from jax.experimental.pallas import tpu as pltpu

F32 = jnp.float32
BF16 = jnp.bfloat16

HEAD_DIM = 128
ATT_HEADS = 8
ATT_KV_HEADS = 2
ATT_GROUP = ATT_HEADS // ATT_KV_HEADS
ATT_WINDOW = 128
RET_HEADS = 8
SG_GROUPS = 8
HGRN_HEADS = 8
CHUNK = 128
SUB = 4
_HGRN_LEVELS = (CHUNK // SUB).bit_length() - 1
HGRN_UNROLL = 4
RET_UNROLL = 8
ATT_UNROLL = 4
BRANCH_W = 8 * HEAD_DIM
LN_EPS = 1e-5

_OFF_AQ = 0
_OFF_AK = _OFF_AQ + ATT_HEADS
_OFF_AV = _OFF_AK + ATT_KV_HEADS
_OFF_RQ = _OFF_AV + ATT_KV_HEADS
_OFF_RK = _OFF_RQ + RET_HEADS
_OFF_RV = _OFF_RK + RET_HEADS
_OFF_RG = _OFF_RV + RET_HEADS
_OFF_SU = _OFF_RG + RET_HEADS
_OFF_SV = _OFF_SU + SG_GROUPS
_OFF_DQ = _OFF_SV + SG_GROUPS
_OFF_DFF = _OFF_DQ + HGRN_HEADS
_OFF_DFB = _OFF_DFF + HGRN_HEADS
_OFF_DI = _OFF_DFB + HGRN_HEADS
_OFF_DG = _OFF_DI + HGRN_HEADS

VMEM_LIMIT_BYTES_V7X = 56 * 1024 * 1024


def _params(n_axes):
    return pltpu.CompilerParams(dimension_semantics=("arbitrary",) * n_axes,
                                vmem_limit_bytes=VMEM_LIMIT_BYTES_V7X)


def _dot(a, b):
    return jnp.dot(a, b, preferred_element_type=F32)


def _dot_nt(a, b):
    return lax.dot_general(a, b, (((1,), (1,)), ((), ())), preferred_element_type=F32)


def _dot_tn(a, b):
    return lax.dot_general(a, b, (((0,), (0,)), ((), ())), preferred_element_type=F32)


def _sigmoid(x):
    return 1.0 / (1.0 + jnp.exp(-x))


def _silu(x):
    return x * _sigmoid(x)


def _gelu_tanh(x):
    return 0.5 * x * (1.0 + jnp.tanh(0.7978845608028654 * (x + 0.044715 * (x * x * x))))


def _pick(n, prefs):
    for p in prefs:
        if n % p == 0:
            return p
    return n


def _mod_kernel(c_ref, w_ref, b_ref, o_ref):
    ca = _silu(c_ref[...]).astype(BF16)
    o_ref[...] = _dot(ca, w_ref[...].astype(BF16)) + b_ref[...]


def _mod_call(c_pad, w_mod, b_mod):
    depth, d, n = w_mod.shape
    rows = c_pad.shape[0]
    tn = _pick(n, (512, 256, 128))
    return pl.pallas_call(
        _mod_kernel,
        out_shape=jax.ShapeDtypeStruct((depth, rows, n), F32),
        grid=(depth, n // tn),
        in_specs=[pl.BlockSpec((rows, d), lambda l, j: (0, 0)),
                  pl.BlockSpec((None, d, tn), lambda l, j: (l, 0, j)),
                  pl.BlockSpec((None, 1, tn), lambda l, j: (l, 0, j))],
        out_specs=pl.BlockSpec((None, rows, tn), lambda l, j: (l, 0, j)),
        compiler_params=_params(2),
        name="adaln_mod",
    )(c_pad, w_mod, b_mod.reshape(depth, 1, n))


def _modulate_kernel(x_ref, sc_ref, sh_ref, o_ref):
    o_ref[...] = (x_ref[...] * (1.0 + sc_ref[...]) + sh_ref[...]).astype(o_ref.dtype)


def _modulate_call(x, sc, sh):
    b, s, d = x.shape
    ts = _pick(s, (512, 256, 128))
    return pl.pallas_call(
        _modulate_kernel,
        out_shape=jax.ShapeDtypeStruct((b, s, d), BF16),
        grid=(b, s // ts),
        in_specs=[pl.BlockSpec((None, ts, d), lambda i, j: (i, j, 0)),
                  pl.BlockSpec((None, 1, d), lambda i, j: (i, 0, 0)),
                  pl.BlockSpec((None, 1, d), lambda i, j: (i, 0, 0))],
        out_specs=pl.BlockSpec((None, ts, d), lambda i, j: (i, j, 0)),
        compiler_params=_params(2),
        name="modulate",
    )(x, sc, sh)


def _mm_kernel(x_ref, w_ref, o_ref):
    o_ref[...] = _dot(x_ref[...], w_ref[...].astype(BF16)).astype(o_ref.dtype)


def _matmul(x, w, layer, *, tm, tn, out_dtype, name):
    m, k = x.shape
    n = w.shape[2]
    return pl.pallas_call(
        _mm_kernel,
        out_shape=jax.ShapeDtypeStruct((m, n), out_dtype),
        grid=(m // tm, n // tn),
        in_specs=[pl.BlockSpec((tm, k), lambda i, j: (i, 0)),
                  pl.BlockSpec((None, k, tn), lambda i, j: (layer, 0, j))],
        out_specs=pl.BlockSpec((tm, tn), lambda i, j: (i, j)),
        compiler_params=_params(2),
        name=name,
    )(x, w)


def _merge_kernel(h_ref, ya_ref, yb_ref, yc_ref, yd_ref, wg_ref, wb_ref, o_ref):
    h = h_ref[...]
    acc = None
    for i, y_ref in enumerate((ya_ref, yb_ref, yc_ref, yd_ref)):
        gate = _sigmoid(_dot(h, wg_ref[i]))
        term = gate * _dot(y_ref[...], wb_ref[i])
        acc = term if acc is None else acc + term
    o_ref[...] = acc.astype(o_ref.dtype)


def _merge_call(h, ys, wg, wb, layer):
    m, d = h.shape
    bw = ys[0].shape[1]
    n = wg.shape[3]
    tm = _pick(m, (512, 256, 128))
    tn = _pick(n, (256, 128))
    y_spec = pl.BlockSpec((tm, bw), lambda j, i: (i, 0))
    return pl.pallas_call(
        _merge_kernel,
        out_shape=jax.ShapeDtypeStruct((m, n), BF16),
        grid=(n // tn, m // tm),
        in_specs=[pl.BlockSpec((tm, d), lambda j, i: (i, 0)), y_spec, y_spec, y_spec, y_spec,
                  pl.BlockSpec((None, 4, d, tn), lambda j, i: (layer, 0, 0, j)),
                  pl.BlockSpec((None, 4, bw, tn), lambda j, i: (layer, 0, 0, j))],
        out_specs=pl.BlockSpec((tm, tn), lambda j, i: (i, j)),
        compiler_params=_params(2),
        name="gated_merge",
    )(h, *ys, wg, wb)


def _residual_ln(x, m, gate, lg, lb, alpha):
    z = alpha * x + (1.0 + gate) * m
    mu = jnp.mean(z, axis=-1, keepdims=True)
    zc = z - mu
    var = jnp.mean(zc * zc, axis=-1, keepdims=True)
    return zc * lax.rsqrt(var + LN_EPS) * lg + lb


def _ln_mod_kernel(x_ref, m_ref, gate_ref, lg_ref, lb_ref, sc_ref, sh_ref, xo_ref, ho_ref, *, alpha):
    xn = _residual_ln(x_ref[...], m_ref[...], gate_ref[...], lg_ref[...], lb_ref[...], alpha)
    xo_ref[...] = xn
    ho_ref[...] = (xn * (1.0 + sc_ref[...]) + sh_ref[...]).astype(ho_ref.dtype)


def _ln_kernel(x_ref, m_ref, gate_ref, lg_ref, lb_ref, xo_ref, *, alpha):
    xo_ref[...] = _residual_ln(x_ref[...], m_ref[...], gate_ref[...], lg_ref[...], lb_ref[...], alpha)


def _ln_call(x, m, gate, ln_g, ln_b, alpha, sc=None, sh=None):
    b, s, d = x.shape
    ts = _pick(s, (256, 128))
    big = pl.BlockSpec((None, ts, d), lambda i, j: (i, j, 0))
    per_b = pl.BlockSpec((None, 1, d), lambda i, j: (i, 0, 0))
    shared = pl.BlockSpec((1, d), lambda i, j: (0, 0))
    x_shape = jax.ShapeDtypeStruct((b, s, d), F32)
    if sc is None:
        return pl.pallas_call(
            functools.partial(_ln_kernel, alpha=alpha),
            out_shape=x_shape,
            grid=(b, s // ts),
            in_specs=[big, big, per_b, shared, shared],
            out_specs=big,
            compiler_params=_params(2),
            name="residual_ln_last",
        )(x, m, gate, ln_g, ln_b), None
    return pl.pallas_call(
        functools.partial(_ln_mod_kernel, alpha=alpha),
        out_shape=(x_shape, jax.ShapeDtypeStruct((b, s, d), BF16)),
        grid=(b, s // ts),
        in_specs=[big, big, per_b, shared, shared, per_b, per_b],
        out_specs=(big, big),
        compiler_params=_params(2),
        name="residual_ln",
    )(x, m, gate, ln_g, ln_b, sc, sh)


def _ffn_in_kernel(h_ref, wa_ref, wg_ref, o_ref):
    h = h_ref[...]
    a = _dot(h, wa_ref[...].astype(BF16))
    g = _dot(h, wg_ref[...].astype(BF16))
    o_ref[...] = (_silu(a) * g).astype(o_ref.dtype)


def _ffn_in_call(h, w, layer, tm):
    m, d = h.shape
    f = w.shape[2] // 2
    tn = _pick(f, (256, 128))
    nj = f // tn
    return pl.pallas_call(
        _ffn_in_kernel,
        out_shape=jax.ShapeDtypeStruct((m, f), BF16),
        grid=(m // tm, nj),
        in_specs=[pl.BlockSpec((tm, d), lambda i, j: (i, 0), pipeline_mode=pl.Buffered(1)),
                  pl.BlockSpec((None, d, tn), lambda i, j: (layer, 0, j)),
                  pl.BlockSpec((None, d, tn), lambda i, j: (layer, 0, j + nj))],
        out_specs=pl.BlockSpec((tm, tn), lambda i, j: (i, j)),
        compiler_params=_params(2),
        name="swiglu_in",
    )(h, w, w)


def _attn_kernel(slope_ref, sink_ref, q_ref, k_ref, v_ref, o_ref, *, seq):
    hkv = pl.program_id(1)
    span = 3 * CHUNK
    scale = HEAD_DIM ** -0.5

    def heads(lo):
        return [slice(lo + g * HEAD_DIM, lo + (g + 1) * HEAD_DIM) for g in range(ATT_GROUP)]

    def blocks(it, carry):
        q0s, vbs, scs = [], [], []
        for u in range(ATT_UNROLL):
            q0 = pl.multiple_of((it * ATT_UNROLL + u) * CHUNK, CHUNK)
            ks = pl.multiple_of(jnp.clip(q0 - CHUNK, 0, seq - span), CHUNK)
            kb = k_ref[pl.ds(ks, span), :].astype(BF16)
            vbs.append(v_ref[pl.ds(ks, span), :].astype(BF16))
            qs = jnp.concatenate([q_ref[pl.ds(q0, CHUNK), sl] for sl in heads(0)], axis=0).astype(BF16)
            scs.append((_dot_nt(qs, kb) * scale, ks))
            q0s.append(q0)
        ps, dens = [], []
        for u in range(ATT_UNROLL):
            sc_all, ks = scs[u]
            qpos = q0s[u] + lax.broadcasted_iota(jnp.int32, (CHUNK, span), 0)
            kpos = ks + lax.broadcasted_iota(jnp.int32, (CHUNK, span), 1)
            dist = jnp.abs(qpos - kpos)
            valid = dist <= ATT_WINDOW
            distf = dist.astype(F32)
            pg, dg = [], []
            for g in range(ATT_GROUP):
                head = hkv * ATT_GROUP + g
                sc = sc_all[g * CHUNK:(g + 1) * CHUNK] - slope_ref[head] * distf
                sc = jnp.where(valid, sc, -jnp.inf)
                sink = sink_ref[head]
                mx = jnp.maximum(jnp.max(sc, axis=-1, keepdims=True), sink)
                p = jnp.exp(sc - mx)
                dg.append(jnp.sum(p, axis=-1, keepdims=True) + jnp.exp(sink - mx))
                pg.append(p.astype(BF16))
            ps.append(jnp.concatenate(pg, axis=0))
            dens.append(dg)
        for u in range(ATT_UNROLL):
            o_all = _dot(ps[u], vbs[u])
            for g, sl in enumerate(heads(0)):
                o = o_all[g * CHUNK:(g + 1) * CHUNK] / dens[u][g]
                o_ref[pl.ds(q0s[u], CHUNK), sl] = o.astype(o_ref.dtype)
        return carry

    lax.fori_loop(0, seq // (CHUNK * ATT_UNROLL), blocks, 0)


def _attn_call(proj, slopes, sink):
    b, s, _ = proj.shape
    qw = ATT_GROUP * HEAD_DIM
    smem = pl.BlockSpec(memory_space=pltpu.SMEM)
    return pl.pallas_call(
        functools.partial(_attn_kernel, seq=s),
        out_shape=jax.ShapeDtypeStruct((b, s, BRANCH_W), BF16),
        grid=(b, ATT_KV_HEADS),
        in_specs=[smem, smem,
                  pl.BlockSpec((None, s, qw), lambda i, h: (i, 0, h)),
                  pl.BlockSpec((None, s, HEAD_DIM), lambda i, h: (i, 0, _OFF_AK + h)),
                  pl.BlockSpec((None, s, HEAD_DIM), lambda i, h: (i, 0, _OFF_AV + h))],
        out_specs=pl.BlockSpec((None, s, qw), lambda i, h: (i, 0, h)),
        compiler_params=_params(2),
        name="windowed_gqa",
    )(slopes, sink, proj, proj, proj)


def _ret_kernel(lg_ref, q_ref, k_ref, v_ref, g_ref, o_ref, sf_ref, *, seq):
    head = pl.program_id(1)
    lg = lg_ref[head]
    nc = seq // CHUNK
    c = CHUNK
    row = lax.broadcasted_iota(jnp.int32, (c, c), 0)
    col = lax.broadcasted_iota(jnp.int32, (c, c), 1)
    dmat = jnp.exp(lg * jnp.abs(row - col).astype(F32))
    pos = lax.broadcasted_iota(jnp.int32, (c, 1), 0).astype(F32)
    kdec_f = jnp.exp(lg * (c - 1.0 - pos))
    kdec_b = jnp.exp(lg * pos)
    qdec_f = jnp.exp(lg * (pos + 1.0))
    qdec_b = jnp.exp(lg * (c - pos))
    cdec = jnp.exp(lg * c)
    kscale = HEAD_DIM ** -0.5

    def fwd(n, st):
        r0 = pl.multiple_of(n * c, c)
        sf_ref[n] = st
        kc = k_ref[pl.ds(r0, c), :] * kscale
        vc = v_ref[pl.ds(r0, c), :].astype(BF16)
        return st * cdec + _dot_tn((kc * kdec_f).astype(BF16), vc)

    lax.fori_loop(0, nc, fwd, jnp.zeros((c, c), F32), unroll=RET_UNROLL)

    def bwd(i, st):
        n = nc - 1 - i
        r0 = pl.multiple_of(n * c, c)
        qc = q_ref[pl.ds(r0, c), :]
        kc = k_ref[pl.ds(r0, c), :] * kscale
        vc = v_ref[pl.ds(r0, c), :].astype(BF16)
        p = _dot_nt(qc.astype(BF16), kc.astype(BF16)) * dmat
        o = _dot(p.astype(BF16), vc)
        o = o + _dot((qc * qdec_f).astype(BF16), sf_ref[n].astype(BF16))
        o = o + _dot((qc * qdec_b).astype(BF16), st.astype(BF16))
        mu = jnp.mean(o, axis=-1, keepdims=True)
        oc = o - mu
        var = jnp.mean(oc * oc, axis=-1, keepdims=True)
        y = oc * lax.rsqrt(var + LN_EPS) * _silu(g_ref[pl.ds(r0, c), :])
        o_ref[pl.ds(r0, c), :] = y.astype(o_ref.dtype)
        return st * cdec + _dot_tn((kc * kdec_b).astype(BF16), vc)

    lax.fori_loop(0, nc, bwd, jnp.zeros((c, c), F32), unroll=RET_UNROLL)


def _ret_call(proj, log_gamma):
    b, s, _ = proj.shape

    def col(off):
        return pl.BlockSpec((None, s, HEAD_DIM), lambda i, h: (i, 0, off + h))

    return pl.pallas_call(
        functools.partial(_ret_kernel, seq=s),
        out_shape=jax.ShapeDtypeStruct((b, s, BRANCH_W), BF16),
        grid=(b, RET_HEADS),
        in_specs=[pl.BlockSpec(memory_space=pltpu.SMEM), col(_OFF_RQ), col(_OFF_RK), col(_OFF_RV), col(_OFF_RG)],
        out_specs=pl.BlockSpec((None, s, HEAD_DIM), lambda i, h: (i, 0, h)),
        scratch_shapes=[pltpu.VMEM((s // CHUNK, CHUNK, CHUNK), F32)],
        compiler_params=_params(2),
        name="retention",
    )(log_gamma, proj, proj, proj, proj)


def _sgu_kernel(u0_ref, u1_ref, v0_ref, v1_ref, w_ref, b_ref, o_ref):
    half = u0_ref.shape[-1]
    v = jnp.concatenate([_gelu_tanh(v0_ref[...]), _gelu_tanh(v1_ref[...])], axis=-1)
    mu = jnp.mean(v, axis=-1, keepdims=True)
    vc = v - mu
    var = jnp.mean(vc * vc, axis=-1, keepdims=True)
    vn = (vc * lax.rsqrt(var + LN_EPS)).astype(BF16)
    nchunk = vn.shape[0] // CHUNK
    for g in range(SG_GROUPS):
        lo = g * HEAD_DIM
        vg = jnp.concatenate([vn[r * CHUNK:(r + 1) * CHUNK, lo:lo + HEAD_DIM] for r in range(nchunk)], axis=-1)
        mixed = _dot(w_ref[g], vg) + b_ref[:, g:g + 1]
        u_ref = u0_ref if lo < half else u1_ref
        ul = lo % half
        for r in range(nchunk):
            u = _gelu_tanh(u_ref[r * CHUNK:(r + 1) * CHUNK, ul:ul + HEAD_DIM])
            o_ref[r * CHUNK:(r + 1) * CHUNK, lo:lo + HEAD_DIM] = (
                u * mixed[:, r * HEAD_DIM:(r + 1) * HEAD_DIM]).astype(o_ref.dtype)


def _sgu_call(proj, sg_w, sg_bt):
    b, s, _ = proj.shape
    half = SG_GROUPS * HEAD_DIM // 2
    hb = half // HEAD_DIM
    rows = _pick(s, (4 * CHUNK, 2 * CHUNK, CHUNK))

    def blk(off):
        return pl.BlockSpec((None, rows, half), lambda i, n: (i, n, off))

    return pl.pallas_call(
        _sgu_kernel,
        out_shape=jax.ShapeDtypeStruct((b, s, BRANCH_W), BF16),
        grid=(b, s // rows),
        in_specs=[blk(_OFF_SU // hb), blk(_OFF_SU // hb + 1), blk(_OFF_SV // hb), blk(_OFF_SV // hb + 1),
                  pl.BlockSpec((SG_GROUPS, CHUNK, CHUNK), lambda i, n: (0, 0, 0)),
                  pl.BlockSpec((CHUNK, SG_GROUPS), lambda i, n: (0, 0))],
        out_specs=pl.BlockSpec((None, rows, BRANCH_W), lambda i, n: (i, n, 0)),
        compiler_params=_params(2),
        name="spatial_gating",
    )(proj, proj, proj, proj, sg_w, sg_bt)


def _split3(x):
    x1 = x.astype(BF16)
    r1 = x - x1.astype(F32)
    x2 = r1.astype(BF16)
    x3 = (r1 - x2.astype(F32)).astype(BF16)
    return x1, x2, x3


def _hgrn_tables(forward):
    c = CHUNK
    row = np.arange(c)[:, None]
    col = np.arange(c)[None, :]
    lmat = ((col <= row) if forward else (col >= row)).astype(np.float32)
    qms, lms = [], []
    b = c // 2
    while b >= SUB:
        upper_r = (row % (2 * b)) >= b
        upper_c = (col % (2 * b)) >= b
        q_side = upper_r if forward else ~upper_r
        k_side = ~upper_c if forward else upper_c
        qms.append(np.broadcast_to(q_side, (c, HEAD_DIM)).astype(np.float32))
        lms.append((q_side & k_side & ((row // (2 * b)) == (col // (2 * b)))).astype(np.float32))
        b //= 2
    off = np.where((row // SUB) == (col // SUB), col - row, 2 * c).astype(np.int32)
    return lmat, np.stack(qms), np.stack(lms), off


def _hgrn_chunks(qzv, lb, st, tmat, qmask_ref, lmask_ref, off3, forward):
    c = CHUNK
    tiles = (c // 8, 8, c)
    gates = []
    for q, z, v in qzv:
        f = lb + (1.0 - lb) * _sigmoid(z)
        g1, g2, g3 = _split3(jnp.log2(f))
        a = _dot(tmat, g1) + _dot(tmat, g2) + _dot(tmat, g3)
        gates.append((1.0 - f, a))

    pairs = []
    for (q, z, v), (kk, a) in zip(qzv, gates):
        dqk = q - kk
        p = None
        for li in range(_HGRN_LEVELS):
            b = c >> (li + 1)
            a4 = a.reshape(c // (2 * b), 2 * b, c)
            mid = a4[:, b - 1:b, :] if forward else a4[:, b:b + 1, :]
            d = (a4 - mid).reshape(c, c)
            u = ((kk + qmask_ref[li] * dqk) * jnp.exp2(-jnp.abs(d))).astype(BF16)
            pl_ = _dot_nt(u, u) * lmask_ref[li]
            p = pl_ if p is None else p + pl_

        a3, k3, q3 = a.reshape(tiles), kk.reshape(tiles), q.reshape(tiles)
        p3 = p.reshape(tiles)
        for delta in range(SUB):
            if delta == 0:
                w = q3 * k3
            else:
                sh = delta if forward else 8 - delta
                a_s = pltpu.roll(a3, sh, 1)
                k_s = pltpu.roll(k3, sh, 1)
                w = q3 * k_s * jnp.exp2(jnp.minimum(a3 - a_s, 0.0))
            rs = jnp.sum(w, axis=-1, keepdims=True)
            p3 = jnp.where(off3 == (-delta if forward else delta), rs, p3)
        pairs.append(p3.reshape(c, c).astype(BF16))

    outs = []
    for (q, z, v), (kk, a), p in zip(qzv, gates, pairs):
        a_end = a[c - 1:c] if forward else a[0:1]
        vb = v.astype(BF16)
        o = _dot_nt((q * jnp.exp2(a)).astype(BF16), st.astype(BF16)) + _dot(p, vb)
        st = st * jnp.exp2(a_end) + _dot_tn(vb, (kk * jnp.exp2(a_end - a)).astype(BF16))
        outs.append(o)
    return outs, st


def _hgrn_kernel(q_ref, zf_ref, zb_ref, v_ref, g_ref, lb_ref, tmat_ref, qmask_ref, lmask_ref, off_ref,
                 o_ref, of_ref, *, seq):
    nc = seq // CHUNK
    c = CHUNK
    lb = lb_ref[...]
    off3 = off_ref[...].reshape(c // 8, 8, c)

    def rows(n):
        return pl.ds(pl.multiple_of(n * c, c), c)

    def fwd(it, st):
        ns = [it * HGRN_UNROLL + u for u in range(HGRN_UNROLL)]
        qzv = [(q_ref[rows(n), :], zf_ref[rows(n), :], v_ref[rows(n), :]) for n in ns]
        outs, st = _hgrn_chunks(qzv, lb, st, tmat_ref[0], qmask_ref.at[0], lmask_ref.at[0], off3, True)
        for n, o in zip(ns, outs):
            of_ref[rows(n), :] = o
        return st

    lax.fori_loop(0, nc // HGRN_UNROLL, fwd, jnp.zeros((c, c), F32))

    def bwd(it, st):
        ns = [nc - 1 - (it * HGRN_UNROLL + u) for u in range(HGRN_UNROLL)]
        qzv = [(q_ref[rows(n), :], zb_ref[rows(n), :], v_ref[rows(n), :]) for n in ns]
        outs, st = _hgrn_chunks(qzv, lb, st, tmat_ref[1], qmask_ref.at[1], lmask_ref.at[1], off3, False)
        for n, o in zip(ns, outs):
            o = o + of_ref[rows(n), :]
            y = o * lax.rsqrt(jnp.mean(o * o, axis=-1, keepdims=True) + LN_EPS) * _silu(g_ref[rows(n), :])
            o_ref[rows(n), :] = y.astype(o_ref.dtype)
        return st

    lax.fori_loop(0, nc // HGRN_UNROLL, bwd, jnp.zeros((c, c), F32))


def _hgrn_call(proj, lb):
    b, s, _ = proj.shape
    tf, tb = _hgrn_tables(True), _hgrn_tables(False)
    tmat = jnp.asarray(np.stack([tf[0], tb[0]]), BF16)
    qmask = jnp.asarray(np.stack([tf[1], tb[1]]))
    lmask = jnp.asarray(np.stack([tf[2], tb[2]]))
    off = jnp.asarray(tf[3])

    def col(off_):
        return pl.BlockSpec((None, s, HEAD_DIM), lambda i, h: (i, 0, off_ + h))

    def whole(arr):
        return pl.BlockSpec(arr.shape, lambda i, h: (0,) * arr.ndim)

    return pl.pallas_call(
        functools.partial(_hgrn_kernel, seq=s),
        out_shape=jax.ShapeDtypeStruct((b, s, BRANCH_W), BF16),
        grid=(b, HGRN_HEADS),
        in_specs=[col(_OFF_DQ), col(_OFF_DFF), col(_OFF_DFB), col(_OFF_DI), col(_OFF_DG),
                  pl.BlockSpec((1, HEAD_DIM), lambda i, h: (0, h)),
                  whole(tmat), whole(qmask), whole(lmask), whole(off)],
        out_specs=pl.BlockSpec((None, s, HEAD_DIM), lambda i, h: (i, 0, h)),
        scratch_shapes=[pltpu.VMEM((s, HEAD_DIM), F32)],
        compiler_params=_params(2),
        name="hgrn2",
    )(proj, proj, proj, proj, proj, lb, tmat, qmask, lmask, off)


def kernel(x, c, w_in, attn_sink, sg_w, sg_b, hgrn_lb_logits, w_branch, w_gate, w_o, w_mod, b_mod, ln_g, ln_b, w_ffn_in, w_ffn_out):
    bsz, seq, d = x.shape
    depth = w_in.shape[0]
    m = bsz * seq
    alpha = (2.0 * depth) ** 0.25

    pz = jax.nn.softmax(hgrn_lb_logits.astype(F32), axis=0)
    lower_bounds = jnp.cumsum(pz, axis=0) - pz[:1]
    slopes = jnp.exp2(-8.0 * jnp.arange(1, ATT_HEADS + 1, dtype=F32) / ATT_HEADS)
    log_gamma = jnp.log1p(-jnp.exp2(-5.0 - jnp.arange(RET_HEADS, dtype=F32)))

    rows = 8
    c_pad = jnp.zeros((rows, d), F32).at[:bsz].set(c)
    mod = _mod_call(c_pad, w_mod, b_mod)[:, :bsz]

    def mod_part(l, i):
        return mod[l, :, i * d:(i + 1) * d].reshape(bsz, 1, d)

    tm = _pick(m, (1024, 512, 256, 128))
    tn = _pick(d, (512, 256, 128))
    wg_bf, wb_bf, wfo_bf = w_gate.astype(BF16), w_branch.astype(BF16), w_ffn_out.astype(BF16)
    h = _modulate_call(x, mod_part(0, 1), mod_part(0, 0))
    for l in range(depth):
        h2d = h.reshape(m, d)
        proj = _matmul(h2d, w_in, l, tm=tm, tn=_pick(w_in.shape[2], (512, 256, 128)),
                       out_dtype=F32, name="in_proj").reshape(bsz, seq, -1)
        y_a = _attn_call(proj, slopes, attn_sink[l].astype(F32))
        y_b = _ret_call(proj, log_gamma)
        y_c = _sgu_call(proj, sg_w[l].astype(BF16), sg_b[l].T)
        y_d = _hgrn_call(proj, lower_bounds[l].reshape(1, -1))
        ys = [y.reshape(m, BRANCH_W) for y in (y_a, y_b, y_c, y_d)]
        merged = _merge_call(h2d, ys, wg_bf, wb_bf, l)
        mix = _matmul(merged, w_o, l, tm=tm, tn=tn, out_dtype=F32, name="out_proj").reshape(bsz, seq, d)
        x, h = _ln_call(x, mix, mod_part(l, 2), ln_g[l, 0:1], ln_b[l, 0:1], alpha, mod_part(l, 4), mod_part(l, 3))
        act = _ffn_in_call(h.reshape(m, d), w_ffn_in, l, _pick(m, (2048, 1024, 512, 256, 128)))
        ffn = _matmul(act, wfo_bf, l, tm=_pick(m, (512, 256, 128)), tn=tn,
                      out_dtype=F32, name="ffn_out").reshape(bsz, seq, d)
        if l + 1 < depth:
            x, h = _ln_call(x, ffn, mod_part(l, 5), ln_g[l, 1:2], ln_b[l, 1:2], alpha,
                            mod_part(l + 1, 1), mod_part(l + 1, 0))
        else:
            x, _ = _ln_call(x, ffn, mod_part(l, 5), ln_g[l, 1:2], ln_b[l, 1:2], alpha)
    return x
```

```python
import functools

import jax
import jax.numpy as jnp
import numpy as np
from jax import lax
from jax.experimental import pallas as pl
from jax.experimental.pallas import tpu as pltpu

F32 = jnp.float32
BF16 = jnp.bfloat16

HEAD_DIM = 128
ATT_HEADS = 8
ATT_KV_HEADS = 2
ATT_GROUP = ATT_HEADS // ATT_KV_HEADS
ATT_WINDOW = 128
RET_HEADS = 8
SG_GROUPS = 8
HGRN_HEADS = 8
CHUNK = 128
SUB = 4
_HGRN_LEVELS = (CHUNK // SUB).bit_length() - 1
HGRN_UNROLL = 4
RET_UNROLL = 8
ATT_UNROLL = 4
BRANCH_W = 8 * HEAD_DIM
N_BRANCH = 4
LN_EPS = 1e-5

_OFF_AQ = 0
_OFF_AK = _OFF_AQ + ATT_HEADS
_OFF_AV = _OFF_AK + ATT_KV_HEADS
_OFF_RQ = _OFF_AV + ATT_KV_HEADS
_OFF_RK = _OFF_RQ + RET_HEADS
_OFF_RV = _OFF_RK + RET_HEADS
_OFF_RG = _OFF_RV + RET_HEADS
_OFF_SU = _OFF_RG + RET_HEADS
_OFF_SV = _OFF_SU + SG_GROUPS
_OFF_DQ = _OFF_SV + SG_GROUPS
_OFF_DFF = _OFF_DQ + HGRN_HEADS
_OFF_DFB = _OFF_DFF + HGRN_HEADS
_OFF_DI = _OFF_DFB + HGRN_HEADS
_OFF_DG = _OFF_DI + HGRN_HEADS

VMEM_LIMIT_BYTES_V7X = 56 * 1024 * 1024


def _params(n_axes):
    return pltpu.CompilerParams(dimension_semantics=("arbitrary",) * n_axes,
                                vmem_limit_bytes=VMEM_LIMIT_BYTES_V7X)


def _dot(a, b):
    return jnp.dot(a, b, preferred_element_type=F32)


def _dot_nt(a, b):
    return lax.dot_general(a, b, (((1,), (1,)), ((), ())), preferred_element_type=F32)


def _dot_tn(a, b):
    return lax.dot_general(a, b, (((0,), (0,)), ((), ())), preferred_element_type=F32)


def _sigmoid(x):
    return 1.0 / (1.0 + jnp.exp(-x))


def _silu(x):
    return x * _sigmoid(x)


def _gelu_tanh(x):
    return 0.5 * x * (1.0 + jnp.tanh(0.7978845608028654 * (x + 0.044715 * (x * x * x))))


def _pick(n, prefs):
    for p in prefs:
        if n % p == 0:
            return p
    return n


def _mod_kernel(c_ref, w_ref, b_ref, o_ref):
    ca = _silu(c_ref[...]).astype(BF16)
    o_ref[...] = _dot(ca, w_ref[...].astype(BF16)) + b_ref[...]


def _mod_call(c_pad, w_mod, b_mod):
    depth, d, n = w_mod.shape
    rows = c_pad.shape[0]
    tn = _pick(n, (512, 256, 128))
    return pl.pallas_call(
        _mod_kernel,
        out_shape=jax.ShapeDtypeStruct((depth, rows, n), F32),
        grid=(depth, n // tn),
        in_specs=[pl.BlockSpec((rows, d), lambda l, j: (0, 0)),
                  pl.BlockSpec((None, d, tn), lambda l, j: (l, 0, j)),
                  pl.BlockSpec((None, 1, tn), lambda l, j: (l, 0, j))],
        out_specs=pl.BlockSpec((None, rows, tn), lambda l, j: (l, 0, j)),
        compiler_params=_params(2),
        name="adaln_mod",
    )(c_pad, w_mod, b_mod.reshape(depth, 1, n))


def _modulate_kernel(x_ref, sc_ref, sh_ref, o_ref):
    o_ref[...] = (x_ref[...] * (1.0 + sc_ref[...]) + sh_ref[...]).astype(o_ref.dtype)


def _modulate_call(x, sc, sh):
    b, s, d = x.shape
    ts = _pick(s, (512, 256, 128))
    return pl.pallas_call(
        _modulate_kernel,
        out_shape=jax.ShapeDtypeStruct((b, s, d), BF16),
        grid=(b, s // ts),
        in_specs=[pl.BlockSpec((None, ts, d), lambda i, j: (i, j, 0)),
                  pl.BlockSpec((None, 1, d), lambda i, j: (i, 0, 0)),
                  pl.BlockSpec((None, 1, d), lambda i, j: (i, 0, 0))],
        out_specs=pl.BlockSpec((None, ts, d), lambda i, j: (i, j, 0)),
        compiler_params=_params(2),
        name="modulate",
    )(x, sc, sh)


def _mm_kernel(x_ref, w_ref, o_ref):
    o_ref[...] = _dot(x_ref[...], w_ref[...].astype(BF16)).astype(o_ref.dtype)


def _matmul(x, w, layer, *, tm, tn, out_dtype, name, lhs_buffers=2):
    m, k = x.shape
    n = w.shape[2]
    return pl.pallas_call(
        _mm_kernel,
        out_shape=jax.ShapeDtypeStruct((m, n), out_dtype),
        grid=(m // tm, n // tn),
        in_specs=[pl.BlockSpec((tm, k), lambda i, j: (i, 0), pipeline_mode=pl.Buffered(lhs_buffers)),
                  pl.BlockSpec((None, k, tn), lambda i, j: (layer, 0, j))],
        out_specs=pl.BlockSpec((tm, tn), lambda i, j: (i, j)),
        compiler_params=_params(2),
        name=name,
    )(x, w)


def _merge_kernel(h_ref, y_ref, wg_ref, wb_ref, o_ref, acc_ref):
    @pl.when(pl.program_id(2) == 0)
    def _():
        acc_ref[...] = jnp.zeros_like(acc_ref)

    gate = _sigmoid(_dot(h_ref[...], wg_ref[...].astype(BF16)))
    acc = acc_ref[...] + gate * _dot(y_ref[...], wb_ref[...].astype(BF16))
    acc_ref[...] = acc
    o_ref[...] = acc.astype(o_ref.dtype)


def _merge_call(h, ys, wg, wb, layer):
    m, d = h.shape
    nb, _, bw = ys.shape
    n = wg.shape[3]
    tm = _pick(m, (1024, 512, 256, 128))
    tn = _pick(n, (512, 256, 128))
    return pl.pallas_call(
        _merge_kernel,
        out_shape=jax.ShapeDtypeStruct((m, n), BF16),
        grid=(m // tm, n // tn, nb),
        in_specs=[pl.BlockSpec((tm, d), lambda i, j, r: (i, 0), pipeline_mode=pl.Buffered(1)),
                  pl.BlockSpec((None, tm, bw), lambda i, j, r: (r, i, 0)),
                  pl.BlockSpec((None, None, d, tn), lambda i, j, r: (layer, r, 0, j)),
                  pl.BlockSpec((None, None, bw, tn), lambda i, j, r: (layer, r, 0, j))],
        out_specs=pl.BlockSpec((tm, tn), lambda i, j, r: (i, j)),
        scratch_shapes=[pltpu.VMEM((tm, tn), F32)],
        compiler_params=_params(3),
        name="gated_merge",
    )(h, ys, wg, wb)


def _residual_ln(x, m, gate, lg, lb, alpha):
    z = alpha * x + (1.0 + gate) * m
    mu = jnp.mean(z, axis=-1, keepdims=True)
    zc = z - mu
    var = jnp.mean(zc * zc, axis=-1, keepdims=True)
    return zc * lax.rsqrt(var + LN_EPS) * lg + lb


def _ln_mod_kernel(x_ref, m_ref, gate_ref, lg_ref, lb_ref, sc_ref, sh_ref, xo_ref, ho_ref, *, alpha):
    xn = _residual_ln(x_ref[...], m_ref[...], gate_ref[...], lg_ref[...], lb_ref[...], alpha)
    xo_ref[...] = xn
    ho_ref[...] = (xn * (1.0 + sc_ref[...]) + sh_ref[...]).astype(ho_ref.dtype)


def _ln_kernel(x_ref, m_ref, gate_ref, lg_ref, lb_ref, xo_ref, *, alpha):
    xo_ref[...] = _residual_ln(x_ref[...], m_ref[...], gate_ref[...], lg_ref[...], lb_ref[...], alpha)


def _ln_call(x, m, gate, ln_g, ln_b, alpha, sc=None, sh=None):
    b, s, d = x.shape
    ts = _pick(s, (256, 128))
    big = pl.BlockSpec((None, ts, d), lambda i, j: (i, j, 0))
    per_b = pl.BlockSpec((None, 1, d), lambda i, j: (i, 0, 0))
    shared = pl.BlockSpec((1, d), lambda i, j: (0, 0))
    x_shape = jax.ShapeDtypeStruct((b, s, d), F32)
    if sc is None:
        return pl.pallas_call(
            functools.partial(_ln_kernel, alpha=alpha),
            out_shape=x_shape,
            grid=(b, s // ts),
            in_specs=[big, big, per_b, shared, shared],
            out_specs=big,
            compiler_params=_params(2),
            name="residual_ln_last",
        )(x, m, gate, ln_g, ln_b), None
    return pl.pallas_call(
        functools.partial(_ln_mod_kernel, alpha=alpha),
        out_shape=(x_shape, jax.ShapeDtypeStruct((b, s, d), BF16)),
        grid=(b, s // ts),
        in_specs=[big, big, per_b, shared, shared, per_b, per_b],
        out_specs=(big, big),
        compiler_params=_params(2),
        name="residual_ln",
    )(x, m, gate, ln_g, ln_b, sc, sh)


def _ffn_in_kernel(h_ref, wa_ref, wg_ref, o_ref):
    h = h_ref[...]
    a = _dot(h, wa_ref[...].astype(BF16))
    g = _dot(h, wg_ref[...].astype(BF16))
    o_ref[...] = (_silu(a) * g).astype(o_ref.dtype)


def _ffn_in_call(h, w, layer, tm):
    m, d = h.shape
    f = w.shape[2] // 2
    tn = _pick(f, (256, 128))
    nj = f // tn
    return pl.pallas_call(
        _ffn_in_kernel,
        out_shape=jax.ShapeDtypeStruct((m, f), BF16),
        grid=(m // tm, nj),
        in_specs=[pl.BlockSpec((tm, d), lambda i, j: (i, 0), pipeline_mode=pl.Buffered(1)),
                  pl.BlockSpec((None, d, tn), lambda i, j: (layer, 0, j)),
                  pl.BlockSpec((None, d, tn), lambda i, j: (layer, 0, j + nj))],
        out_specs=pl.BlockSpec((tm, tn), lambda i, j: (i, j)),
        compiler_params=_params(2),
        name="swiglu_in",
    )(h, w, w)


def _attn_kernel(slope_ref, sink_ref, q_ref, k_ref, v_ref, o_ref, *, seq):
    hkv = pl.program_id(1)
    span = 3 * CHUNK
    scale = HEAD_DIM ** -0.5

    def heads(lo):
        return [slice(lo + g * HEAD_DIM, lo + (g + 1) * HEAD_DIM) for g in range(ATT_GROUP)]

    def blocks(it, carry):
        q0s, vbs, scs = [], [], []
        for u in range(ATT_UNROLL):
            q0 = pl.multiple_of((it * ATT_UNROLL + u) * CHUNK, CHUNK)
            ks = pl.multiple_of(jnp.clip(q0 - CHUNK, 0, seq - span), CHUNK)
            kb = k_ref[pl.ds(ks, span), :].astype(BF16)
            vbs.append(v_ref[pl.ds(ks, span), :].astype(BF16))
            qs = jnp.concatenate([q_ref[pl.ds(q0, CHUNK), sl] for sl in heads(0)], axis=0).astype(BF16)
            scs.append((_dot_nt(qs, kb) * scale, ks))
            q0s.append(q0)
        ps, dens = [], []
        for u in range(ATT_UNROLL):
            sc_all, ks = scs[u]
            qpos = q0s[u] + lax.broadcasted_iota(jnp.int32, (CHUNK, span), 0)
            kpos = ks + lax.broadcasted_iota(jnp.int32, (CHUNK, span), 1)
            dist = jnp.abs(qpos - kpos)
            valid = dist <= ATT_WINDOW
            distf = dist.astype(F32)
            pg, dg = [], []
            for g in range(ATT_GROUP):
                head = hkv * ATT_GROUP + g
                sc = sc_all[g * CHUNK:(g + 1) * CHUNK] - slope_ref[head] * distf
                sc = jnp.where(valid, sc, -jnp.inf)
                sink = sink_ref[head]
                mx = jnp.maximum(jnp.max(sc, axis=-1, keepdims=True), sink)
                p = jnp.exp(sc - mx)
                dg.append(jnp.sum(p, axis=-1, keepdims=True) + jnp.exp(sink - mx))
                pg.append(p.astype(BF16))
            ps.append(jnp.concatenate(pg, axis=0))
            dens.append(dg)
        for u in range(ATT_UNROLL):
            o_all = _dot(ps[u], vbs[u])
            for g, sl in enumerate(heads(0)):
                o = o_all[g * CHUNK:(g + 1) * CHUNK] / dens[u][g]
                o_ref[pl.ds(q0s[u], CHUNK), sl] = o.astype(o_ref.dtype)
        return carry

    lax.fori_loop(0, seq // (CHUNK * ATT_UNROLL), blocks, 0)


def _attn_call(proj, slopes, sink):
    b, s, _ = proj.shape
    qw = ATT_GROUP * HEAD_DIM
    smem = pl.BlockSpec(memory_space=pltpu.SMEM)
    return pl.pallas_call(
        functools.partial(_attn_kernel, seq=s),
        out_shape=jax.ShapeDtypeStruct((N_BRANCH, b, s, BRANCH_W), BF16),
        grid=(b, ATT_KV_HEADS),
        in_specs=[smem, smem,
                  pl.BlockSpec((None, s, qw), lambda i, h: (i, 0, h)),
                  pl.BlockSpec((None, s, HEAD_DIM), lambda i, h: (i, 0, _OFF_AK + h)),
                  pl.BlockSpec((None, s, HEAD_DIM), lambda i, h: (i, 0, _OFF_AV + h))],
        out_specs=pl.BlockSpec((None, None, s, qw), lambda i, h: (0, i, 0, h)),
        compiler_params=_params(2),
        name="windowed_gqa",
    )(slopes, sink, proj, proj, proj)


def _ret_kernel(lg_ref, q_ref, k_ref, v_ref, g_ref, ybuf_ref, o_ref, sf_ref, *, seq):
    head = pl.program_id(1)
    lg = lg_ref[head]
    nc = seq // CHUNK
    c = CHUNK
    row = lax.broadcasted_iota(jnp.int32, (c, c), 0)
    col = lax.broadcasted_iota(jnp.int32, (c, c), 1)
    dmat = jnp.exp(lg * jnp.abs(row - col).astype(F32))
    pos = lax.broadcasted_iota(jnp.int32, (c, 1), 0).astype(F32)
    kdec_f = jnp.exp(lg * (c - 1.0 - pos))
    kdec_b = jnp.exp(lg * pos)
    qdec_f = jnp.exp(lg * (pos + 1.0))
    qdec_b = jnp.exp(lg * (c - pos))
    cdec = jnp.exp(lg * c)
    kscale = HEAD_DIM ** -0.5

    def fwd(n, st):
        r0 = pl.multiple_of(n * c, c)
        sf_ref[n] = st
        kc = k_ref[pl.ds(r0, c), :] * kscale
        vc = v_ref[pl.ds(r0, c), :].astype(BF16)
        return st * cdec + _dot_tn((kc * kdec_f).astype(BF16), vc)

    lax.fori_loop(0, nc, fwd, jnp.zeros((c, c), F32), unroll=RET_UNROLL)

    def bwd(i, st):
        n = nc - 1 - i
        r0 = pl.multiple_of(n * c, c)
        qc = q_ref[pl.ds(r0, c), :]
        kc = k_ref[pl.ds(r0, c), :] * kscale
        vc = v_ref[pl.ds(r0, c), :].astype(BF16)
        p = _dot_nt(qc.astype(BF16), kc.astype(BF16)) * dmat
        o = _dot(p.astype(BF16), vc)
        o = o + _dot((qc * qdec_f).astype(BF16), sf_ref[n].astype(BF16))
        o = o + _dot((qc * qdec_b).astype(BF16), st.astype(BF16))
        mu = jnp.mean(o, axis=-1, keepdims=True)
        oc = o - mu
        var = jnp.mean(oc * oc, axis=-1, keepdims=True)
        y = oc * lax.rsqrt(var + LN_EPS) * _silu(g_ref[pl.ds(r0, c), :])
        o_ref[pl.ds(r0, c), :] = y.astype(o_ref.dtype)
        return st * cdec + _dot_tn((kc * kdec_b).astype(BF16), vc)

    lax.fori_loop(0, nc, bwd, jnp.zeros((c, c), F32), unroll=RET_UNROLL)


_IN_PLACE = pl.BlockSpec(memory_space=pl.ANY)


def _ret_call(proj, log_gamma, ybuf):
    b, s, _ = proj.shape

    def col(off):
        return pl.BlockSpec((None, s, HEAD_DIM), lambda i, h: (i, 0, off + h))

    return pl.pallas_call(
        functools.partial(_ret_kernel, seq=s),
        out_shape=jax.ShapeDtypeStruct(ybuf.shape, ybuf.dtype),
        grid=(b, RET_HEADS),
        in_specs=[pl.BlockSpec(memory_space=pltpu.SMEM), col(_OFF_RQ), col(_OFF_RK), col(_OFF_RV), col(_OFF_RG),
                  _IN_PLACE],
        out_specs=pl.BlockSpec((None, None, s, HEAD_DIM), lambda i, h: (1, i, 0, h)),
        scratch_shapes=[pltpu.VMEM((s // CHUNK, CHUNK, CHUNK), F32)],
        input_output_aliases={5: 0},
        compiler_params=_params(2),
        name="retention",
    )(log_gamma, proj, proj, proj, proj, ybuf)


def _sgu_kernel(u0_ref, u1_ref, v0_ref, v1_ref, w_ref, b_ref, ybuf_ref, o_ref):
    half = u0_ref.shape[-1]
    v = jnp.concatenate([_gelu_tanh(v0_ref[...]), _gelu_tanh(v1_ref[...])], axis=-1)
    mu = jnp.mean(v, axis=-1, keepdims=True)
    vc = v - mu
    var = jnp.mean(vc * vc, axis=-1, keepdims=True)
    vn = (vc * lax.rsqrt(var + LN_EPS)).astype(BF16)
    nchunk = vn.shape[0] // CHUNK
    for g in range(SG_GROUPS):
        lo = g * HEAD_DIM
        vg = jnp.concatenate([vn[r * CHUNK:(r + 1) * CHUNK, lo:lo + HEAD_DIM] for r in range(nchunk)], axis=-1)
        mixed = _dot(w_ref[g], vg) + b_ref[:, g:g + 1]
        u_ref = u0_ref if lo < half else u1_ref
        ul = lo % half
        for r in range(nchunk):
            u = _gelu_tanh(u_ref[r * CHUNK:(r + 1) * CHUNK, ul:ul + HEAD_DIM])
            o_ref[r * CHUNK:(r + 1) * CHUNK, lo:lo + HEAD_DIM] = (
                u * mixed[:, r * HEAD_DIM:(r + 1) * HEAD_DIM]).astype(o_ref.dtype)


def _sgu_call(proj, sg_w, sg_bt, ybuf):
    b, s, _ = proj.shape
    half = SG_GROUPS * HEAD_DIM // 2
    hb = half // HEAD_DIM
    rows = _pick(s, (4 * CHUNK, 2 * CHUNK, CHUNK))

    def blk(off):
        return pl.BlockSpec((None, rows, half), lambda i, n: (i, n, off))

    return pl.pallas_call(
        _sgu_kernel,
        out_shape=jax.ShapeDtypeStruct(ybuf.shape, ybuf.dtype),
        grid=(b, s // rows),
        in_specs=[blk(_OFF_SU // hb), blk(_OFF_SU // hb + 1), blk(_OFF_SV // hb), blk(_OFF_SV // hb + 1),
                  pl.BlockSpec((SG_GROUPS, CHUNK, CHUNK), lambda i, n: (0, 0, 0)),
                  pl.BlockSpec((CHUNK, SG_GROUPS), lambda i, n: (0, 0)),
                  _IN_PLACE],
        out_specs=pl.BlockSpec((None, None, rows, BRANCH_W), lambda i, n: (2, i, n, 0)),
        input_output_aliases={6: 0},
        compiler_params=_params(2),
        name="spatial_gating",
    )(proj, proj, proj, proj, sg_w, sg_bt, ybuf)


def _split3(x):
    x1 = x.astype(BF16)
    r1 = x - x1.astype(F32)
    x2 = r1.astype(BF16)
    x3 = (r1 - x2.astype(F32)).astype(BF16)
    return x1, x2, x3


def _hgrn_tables(forward):
    c = CHUNK
    row = np.arange(c)[:, None]
    col = np.arange(c)[None, :]
    lmat = ((col <= row) if forward else (col >= row)).astype(np.float32)
    qms, lms = [], []
    b = c // 2
    while b >= SUB:
        upper_r = (row % (2 * b)) >= b
        upper_c = (col % (2 * b)) >= b
        q_side = upper_r if forward else ~upper_r
        k_side = ~upper_c if forward else upper_c
        qms.append(np.broadcast_to(q_side, (c, HEAD_DIM)).astype(np.float32))
        lms.append((q_side & k_side & ((row // (2 * b)) == (col // (2 * b)))).astype(np.float32))
        b //= 2
    off = np.where((row // SUB) == (col // SUB), col - row, 2 * c).astype(np.int32)
    return lmat, np.stack(qms), np.stack(lms), off


def _hgrn_chunks(qzv, lb, st, tmat, qmask_ref, lmask_ref, off3, forward):
    c = CHUNK
    tiles = (c // 8, 8, c)
    gates = []
    for q, z, v in qzv:
        f = lb + (1.0 - lb) * _sigmoid(z)
        g1, g2, g3 = _split3(jnp.log2(f))
        a = _dot(tmat, g1) + _dot(tmat, g2) + _dot(tmat, g3)
        gates.append((1.0 - f, a))

    pairs = []
    for (q, z, v), (kk, a) in zip(qzv, gates):
        dqk = q - kk
        p = None
        for li in range(_HGRN_LEVELS):
            b = c >> (li + 1)
            a4 = a.reshape(c // (2 * b), 2 * b, c)
            mid = a4[:, b - 1:b, :] if forward else a4[:, b:b + 1, :]
            d = (a4 - mid).reshape(c, c)
            u = ((kk + qmask_ref[li] * dqk) * jnp.exp2(-jnp.abs(d))).astype(BF16)
            pl_ = _dot_nt(u, u) * lmask_ref[li]
            p = pl_ if p is None else p + pl_

        a3, k3, q3 = a.reshape(tiles), kk.reshape(tiles), q.reshape(tiles)
        p3 = p.reshape(tiles)
        for delta in range(SUB):
            if delta == 0:
                w = q3 * k3
            else:
                sh = delta if forward else 8 - delta
                a_s = pltpu.roll(a3, sh, 1)
                k_s = pltpu.roll(k3, sh, 1)
                w = q3 * k_s * jnp.exp2(jnp.minimum(a3 - a_s, 0.0))
            rs = jnp.sum(w, axis=-1, keepdims=True)
            p3 = jnp.where(off3 == (-delta if forward else delta), rs, p3)
        pairs.append(p3.reshape(c, c).astype(BF16))

    outs = []
    for (q, z, v), (kk, a), p in zip(qzv, gates, pairs):
        a_end = a[c - 1:c] if forward else a[0:1]
        vb = v.astype(BF16)
        o = _dot_nt((q * jnp.exp2(a)).astype(BF16), st.astype(BF16)) + _dot(p, vb)
        st = st * jnp.exp2(a_end) + _dot_tn(vb, (kk * jnp.exp2(a_end - a)).astype(BF16))
        outs.append(o)
    return outs, st


def _hgrn_kernel(q_ref, zf_ref, zb_ref, v_ref, g_ref, lb_ref, tmat_ref, qmask_ref, lmask_ref, off_ref,
                 ybuf_ref, o_ref, of_ref, *, seq):
    nc = seq // CHUNK
    c = CHUNK
    lb = lb_ref[...]
    off3 = off_ref[...].reshape(c // 8, 8, c)

    def rows(n):
        return pl.ds(pl.multiple_of(n * c, c), c)

    def fwd(it, st):
        ns = [it * HGRN_UNROLL + u for u in range(HGRN_UNROLL)]
        qzv = [(q_ref[rows(n), :], zf_ref[rows(n), :], v_ref[rows(n), :]) for n in ns]
        outs, st = _hgrn_chunks(qzv, lb, st, tmat_ref[0], qmask_ref.at[0], lmask_ref.at[0], off3, True)
        for n, o in zip(ns, outs):
            of_ref[rows(n), :] = o
        return st

    lax.fori_loop(0, nc // HGRN_UNROLL, fwd, jnp.zeros((c, c), F32))

    def bwd(it, st):
        ns = [nc - 1 - (it * HGRN_UNROLL + u) for u in range(HGRN_UNROLL)]
        qzv = [(q_ref[rows(n), :], zb_ref[rows(n), :], v_ref[rows(n), :]) for n in ns]
        outs, st = _hgrn_chunks(qzv, lb, st, tmat_ref[1], qmask_ref.at[1], lmask_ref.at[1], off3, False)
        for n, o in zip(ns, outs):
            o = o + of_ref[rows(n), :]
            y = o * lax.rsqrt(jnp.mean(o * o, axis=-1, keepdims=True) + LN_EPS) * _silu(g_ref[rows(n), :])
            o_ref[rows(n), :] = y.astype(o_ref.dtype)
        return st

    lax.fori_loop(0, nc // HGRN_UNROLL, bwd, jnp.zeros((c, c), F32))


def _hgrn_call(proj, lb, ybuf):
    b, s, _ = proj.shape
    tf, tb = _hgrn_tables(True), _hgrn_tables(False)
    tmat = jnp.asarray(np.stack([tf[0], tb[0]]), BF16)
    qmask = jnp.asarray(np.stack([tf[1], tb[1]]))
    lmask = jnp.asarray(np.stack([tf[2], tb[2]]))
    off = jnp.asarray(tf[3])

    def col(off_):
        return pl.BlockSpec((None, s, HEAD_DIM), lambda i, h: (i, 0, off_ + h))

    def whole(arr):
        return pl.BlockSpec(arr.shape, lambda i, h: (0,) * arr.ndim)

    return pl.pallas_call(
        functools.partial(_hgrn_kernel, seq=s),
        out_shape=jax.ShapeDtypeStruct(ybuf.shape, ybuf.dtype),
        grid=(b, HGRN_HEADS),
        in_specs=[col(_OFF_DQ), col(_OFF_DFF), col(_OFF_DFB), col(_OFF_DI), col(_OFF_DG),
                  pl.BlockSpec((1, HEAD_DIM), lambda i, h: (0, h)),
                  whole(tmat), whole(qmask), whole(lmask), whole(off), _IN_PLACE],
        out_specs=pl.BlockSpec((None, None, s, HEAD_DIM), lambda i, h: (3, i, 0, h)),
        scratch_shapes=[pltpu.VMEM((s, HEAD_DIM), F32)],
        input_output_aliases={10: 0},
        compiler_params=_params(2),
        name="hgrn2",
    )(proj, proj, proj, proj, proj, lb, tmat, qmask, lmask, off, ybuf)


def kernel(x, c, w_in, attn_sink, sg_w, sg_b, hgrn_lb_logits, w_branch, w_gate, w_o, w_mod, b_mod, ln_g, ln_b, w_ffn_in, w_ffn_out):
    bsz, seq, d = x.shape
    depth = w_in.shape[0]
    m = bsz * seq
    alpha = (2.0 * depth) ** 0.25

    pz = jax.nn.softmax(hgrn_lb_logits.astype(F32), axis=0)
    lower_bounds = jnp.cumsum(pz, axis=0) - pz[:1]
    slopes = jnp.exp2(-8.0 * jnp.arange(1, ATT_HEADS + 1, dtype=F32) / ATT_HEADS)
    log_gamma = jnp.log1p(-jnp.exp2(-5.0 - jnp.arange(RET_HEADS, dtype=F32)))

    rows = 8
    c_pad = jnp.zeros((rows, d), F32).at[:bsz].set(c)
    mod = _mod_call(c_pad, w_mod, b_mod)[:, :bsz]

    def mod_part(l, i):
        return mod[l, :, i * d:(i + 1) * d].reshape(bsz, 1, d)

    tm = _pick(m, (1024, 512, 256, 128))
    tn = _pick(d, (512, 256, 128))
    h = _modulate_call(x, mod_part(0, 1), mod_part(0, 0))
    for l in range(depth):
        h2d = h.reshape(m, d)
        proj = _matmul(h2d, w_in, l, tm=tm, tn=_pick(w_in.shape[2], (512, 256, 128)),
                       out_dtype=F32, name="in_proj").reshape(bsz, seq, -1)
        ys = _attn_call(proj, slopes, attn_sink[l].astype(F32))
        ys = _ret_call(proj, log_gamma, ys)
        ys = _sgu_call(proj, sg_w[l].astype(BF16), sg_b[l].T, ys)
        ys = _hgrn_call(proj, lower_bounds[l].reshape(1, -1), ys)
        merged = _merge_call(h2d, ys.reshape(N_BRANCH, m, BRANCH_W), w_gate, w_branch, l)
        mix = _matmul(merged, w_o, l, tm=tm, tn=tn, out_dtype=BF16, name="out_proj").reshape(bsz, seq, d)
        x, h = _ln_call(x, mix, mod_part(l, 2), ln_g[l, 0:1], ln_b[l, 0:1], alpha, mod_part(l, 4), mod_part(l, 3))
        act = _ffn_in_call(h.reshape(m, d), w_ffn_in, l, _pick(m, (2048, 1024, 512, 256, 128)))
        ffn = _matmul(act, w_ffn_out, l, tm=tm, tn=_pick(d, (256, 128)), out_dtype=BF16, name="ffn_out",
                      lhs_buffers=1).reshape(bsz, seq, d)
        if l + 1 < depth:
            x, h = _ln_call(x, ffn, mod_part(l, 5), ln_g[l, 1:2], ln_b[l, 1:2], alpha,
                            mod_part(l + 1, 1), mod_part(l + 1, 0))
        else:
            x, _ = _ln_call(x, ffn, mod_part(l, 5), ln_g[l, 1:2], ln_b[l, 1:2], alpha)
    return x
```

```python
import functools

import jax
import jax.numpy as jnp
import numpy as np
from jax import lax
from jax.experimental import pallas as pl
from jax.experimental.pallas import tpu as pltpu

F32 = jnp.float32
BF16 = jnp.bfloat16

HEAD_DIM = 128
ATT_HEADS = 8
ATT_KV_HEADS = 2
ATT_GROUP = ATT_HEADS // ATT_KV_HEADS
ATT_WINDOW = 128
RET_HEADS = 8
SG_GROUPS = 8
HGRN_HEADS = 8
CHUNK = 128
SUB = 4
_HGRN_LEVELS = (CHUNK // SUB).bit_length() - 1
HGRN_UNROLL = 8
RET_UNROLL = 8
ATT_UNROLL = 8
BRANCH_W = 8 * HEAD_DIM
N_BRANCH = 4
LN_EPS = 1e-5

_OFF_AQ = 0
_OFF_AK = _OFF_AQ + ATT_HEADS
_OFF_AV = _OFF_AK + ATT_KV_HEADS
_OFF_RQ = _OFF_AV + ATT_KV_HEADS
_OFF_RK = _OFF_RQ + RET_HEADS
_OFF_RV = _OFF_RK + RET_HEADS
_OFF_RG = _OFF_RV + RET_HEADS
_OFF_SU = _OFF_RG + RET_HEADS
_OFF_SV = _OFF_SU + SG_GROUPS
_OFF_DQ = _OFF_SV + SG_GROUPS
_OFF_DFF = _OFF_DQ + HGRN_HEADS
_OFF_DFB = _OFF_DFF + HGRN_HEADS
_OFF_DI = _OFF_DFB + HGRN_HEADS
_OFF_DG = _OFF_DI + HGRN_HEADS

VMEM_LIMIT_BYTES_V7X = 56 * 1024 * 1024


def _params(n_axes):
    return pltpu.CompilerParams(dimension_semantics=("arbitrary",) * n_axes,
                                vmem_limit_bytes=VMEM_LIMIT_BYTES_V7X)


def _dot(a, b):
    return jnp.dot(a, b, preferred_element_type=F32)


def _dot_nt(a, b):
    return lax.dot_general(a, b, (((1,), (1,)), ((), ())), preferred_element_type=F32)


def _dot_tn(a, b):
    return lax.dot_general(a, b, (((0,), (0,)), ((), ())), preferred_element_type=F32)


def _sigmoid(x):
    return 1.0 / (1.0 + jnp.exp(-x))


def _silu(x):
    return x * _sigmoid(x)


def _gelu_tanh(x):
    return 0.5 * x * (1.0 + jnp.tanh(0.7978845608028654 * (x + 0.044715 * (x * x * x))))


def _pick(n, prefs):
    for p in prefs:
        if n % p == 0:
            return p
    return n


def _mod_kernel(c_ref, w_ref, b_ref, o_ref):
    ca = _silu(c_ref[...]).astype(BF16)
    o_ref[...] = _dot(ca, w_ref[...].astype(BF16)) + b_ref[...]


def _mod_call(c_pad, w_mod, b_mod):
    depth, d, n = w_mod.shape
    rows = c_pad.shape[0]
    tn = _pick(n, (512, 256, 128))
    return pl.pallas_call(
        _mod_kernel,
        out_shape=jax.ShapeDtypeStruct((depth, rows, n), F32),
        grid=(depth, n // tn),
        in_specs=[pl.BlockSpec((rows, d), lambda l, j: (0, 0)),
                  pl.BlockSpec((None, d, tn), lambda l, j: (l, 0, j)),
                  pl.BlockSpec((None, 1, tn), lambda l, j: (l, 0, j))],
        out_specs=pl.BlockSpec((None, rows, tn), lambda l, j: (l, 0, j)),
        compiler_params=_params(2),
        name="adaln_mod",
    )(c_pad, w_mod, b_mod.reshape(depth, 1, n))


def _modulate_kernel(x_ref, sc_ref, sh_ref, o_ref):
    o_ref[...] = (x_ref[...] * (1.0 + sc_ref[...]) + sh_ref[...]).astype(o_ref.dtype)


def _modulate_call(x, sc, sh):
    b, s, d = x.shape
    ts = _pick(s, (512, 256, 128))
    return pl.pallas_call(
        _modulate_kernel,
        out_shape=jax.ShapeDtypeStruct((b, s, d), BF16),
        grid=(b, s // ts),
        in_specs=[pl.BlockSpec((None, ts, d), lambda i, j: (i, j, 0)),
                  pl.BlockSpec((None, 1, d), lambda i, j: (i, 0, 0)),
                  pl.BlockSpec((None, 1, d), lambda i, j: (i, 0, 0))],
        out_specs=pl.BlockSpec((None, ts, d), lambda i, j: (i, j, 0)),
        compiler_params=_params(2),
        name="modulate",
    )(x, sc, sh)


def _mm_kernel(x_ref, w_ref, o_ref):
    o_ref[...] = _dot(x_ref[...], w_ref[...].astype(BF16)).astype(o_ref.dtype)


def _matmul(x, w, layer, *, tm, tn, out_dtype, name, lhs_buffers=2):
    m, k = x.shape
    n = w.shape[2]
    return pl.pallas_call(
        _mm_kernel,
        out_shape=jax.ShapeDtypeStruct((m, n), out_dtype),
        grid=(m // tm, n // tn),
        in_specs=[pl.BlockSpec((tm, k), lambda i, j: (i, 0), pipeline_mode=pl.Buffered(lhs_buffers)),
                  pl.BlockSpec((None, k, tn), lambda i, j: (layer, 0, j))],
        out_specs=pl.BlockSpec((tm, tn), lambda i, j: (i, j)),
        compiler_params=_params(2),
        name=name,
    )(x, w)


def _merge_kernel(h_ref, y_ref, wg_ref, wb_ref, o_ref, acc_ref):
    @pl.when(pl.program_id(2) == 0)
    def _():
        acc_ref[...] = jnp.zeros_like(acc_ref)

    gate = _sigmoid(_dot(h_ref[...], wg_ref[...].astype(BF16)))
    acc = acc_ref[...] + gate * _dot(y_ref[...], wb_ref[...].astype(BF16))
    acc_ref[...] = acc
    o_ref[...] = acc.astype(o_ref.dtype)


def _merge_call(h, ys, wg, wb, layer):
    m, d = h.shape
    nb, _, bw = ys.shape
    n = wg.shape[3]
    tm = _pick(m, (1024, 512, 256, 128))
    tn = _pick(n, (512, 256, 128))
    return pl.pallas_call(
        _merge_kernel,
        out_shape=jax.ShapeDtypeStruct((m, n), BF16),
        grid=(m // tm, n // tn, nb),
        in_specs=[pl.BlockSpec((tm, d), lambda i, j, r: (i, 0), pipeline_mode=pl.Buffered(1)),
                  pl.BlockSpec((None, tm, bw), lambda i, j, r: (r, i, 0)),
                  pl.BlockSpec((None, None, d, tn), lambda i, j, r: (layer, r, 0, j)),
                  pl.BlockSpec((None, None, bw, tn), lambda i, j, r: (layer, r, 0, j))],
        out_specs=pl.BlockSpec((tm, tn), lambda i, j, r: (i, j)),
        scratch_shapes=[pltpu.VMEM((tm, tn), F32)],
        compiler_params=_params(3),
        name="gated_merge",
    )(h, ys, wg, wb)


def _residual_ln(x, m, gate, lg, lb, alpha):
    z = alpha * x + (1.0 + gate) * m
    mu = jnp.mean(z, axis=-1, keepdims=True)
    zc = z - mu
    var = jnp.mean(zc * zc, axis=-1, keepdims=True)
    return zc * lax.rsqrt(var + LN_EPS) * lg + lb


def _ln_mod_kernel(x_ref, m_ref, gate_ref, lg_ref, lb_ref, sc_ref, sh_ref, xo_ref, ho_ref, *, alpha):
    xn = _residual_ln(x_ref[...], m_ref[...], gate_ref[...], lg_ref[...], lb_ref[...], alpha)
    xo_ref[...] = xn
    ho_ref[...] = (xn * (1.0 + sc_ref[...]) + sh_ref[...]).astype(ho_ref.dtype)


def _ln_kernel(x_ref, m_ref, gate_ref, lg_ref, lb_ref, xo_ref, *, alpha):
    xo_ref[...] = _residual_ln(x_ref[...], m_ref[...], gate_ref[...], lg_ref[...], lb_ref[...], alpha)


def _ln_call(x, m, gate, ln_g, ln_b, alpha, sc=None, sh=None):
    b, s, d = x.shape
    ts = _pick(s, (256, 128))
    big = pl.BlockSpec((None, ts, d), lambda i, j: (i, j, 0))
    per_b = pl.BlockSpec((None, 1, d), lambda i, j: (i, 0, 0))
    shared = pl.BlockSpec((1, d), lambda i, j: (0, 0))
    x_shape = jax.ShapeDtypeStruct((b, s, d), F32)
    if sc is None:
        return pl.pallas_call(
            functools.partial(_ln_kernel, alpha=alpha),
            out_shape=x_shape,
            grid=(b, s // ts),
            in_specs=[big, big, per_b, shared, shared],
            out_specs=big,
            compiler_params=_params(2),
            name="residual_ln_last",
        )(x, m, gate, ln_g, ln_b), None
    return pl.pallas_call(
        functools.partial(_ln_mod_kernel, alpha=alpha),
        out_shape=(x_shape, jax.ShapeDtypeStruct((b, s, d), BF16)),
        grid=(b, s // ts),
        in_specs=[big, big, per_b, shared, shared, per_b, per_b],
        out_specs=(big, big),
        compiler_params=_params(2),
        name="residual_ln",
    )(x, m, gate, ln_g, ln_b, sc, sh)


def _ffn_in_kernel(h_ref, wa_ref, wg_ref, o_ref):
    h = h_ref[...]
    a = _dot(h, wa_ref[...].astype(BF16))
    g = _dot(h, wg_ref[...].astype(BF16))
    o_ref[...] = (_silu(a) * g).astype(o_ref.dtype)


def _ffn_in_call(h, w, layer, tm):
    m, d = h.shape
    f = w.shape[2] // 2
    tn = _pick(f, (256, 128))
    nj = f // tn
    return pl.pallas_call(
        _ffn_in_kernel,
        out_shape=jax.ShapeDtypeStruct((m, f), BF16),
        grid=(m // tm, nj),
        in_specs=[pl.BlockSpec((tm, d), lambda i, j: (i, 0), pipeline_mode=pl.Buffered(1)),
                  pl.BlockSpec((None, d, tn), lambda i, j: (layer, 0, j)),
                  pl.BlockSpec((None, d, tn), lambda i, j: (layer, 0, j + nj))],
        out_specs=pl.BlockSpec((tm, tn), lambda i, j: (i, j)),
        compiler_params=_params(2),
        name="swiglu_in",
    )(h, w, w)


def _attn_kernel(slope_ref, sink_ref, q_ref, k_ref, v_ref, o_ref, *, seq):
    hkv = pl.program_id(1)
    span = 3 * CHUNK
    scale = HEAD_DIM ** -0.5

    def heads(lo):
        return [slice(lo + g * HEAD_DIM, lo + (g + 1) * HEAD_DIM) for g in range(ATT_GROUP)]

    def blocks(it, carry):
        q0s, vbs, scs = [], [], []
        for u in range(ATT_UNROLL):
            q0 = pl.multiple_of((it * ATT_UNROLL + u) * CHUNK, CHUNK)
            ks = pl.multiple_of(jnp.clip(q0 - CHUNK, 0, seq - span), CHUNK)
            kb = k_ref[pl.ds(ks, span), :].astype(BF16)
            vbs.append(v_ref[pl.ds(ks, span), :].astype(BF16))
            qs = jnp.concatenate([q_ref[pl.ds(q0, CHUNK), sl] for sl in heads(0)], axis=0).astype(BF16)
            scs.append((_dot_nt(qs, kb) * scale, ks))
            q0s.append(q0)
        ps, dens = [], []
        for u in range(ATT_UNROLL):
            sc_all, ks = scs[u]
            qpos = q0s[u] + lax.broadcasted_iota(jnp.int32, (CHUNK, span), 0)
            kpos = ks + lax.broadcasted_iota(jnp.int32, (CHUNK, span), 1)
            dist = jnp.abs(qpos - kpos)
            valid = dist <= ATT_WINDOW
            distf = dist.astype(F32)
            pg, dg = [], []
            for g in range(ATT_GROUP):
                head = hkv * ATT_GROUP + g
                sc = sc_all[g * CHUNK:(g + 1) * CHUNK] - slope_ref[head] * distf
                sc = jnp.where(valid, sc, -jnp.inf)
                sink = sink_ref[head]
                mx = jnp.maximum(jnp.max(sc, axis=-1, keepdims=True), sink)
                p = jnp.exp(sc - mx)
                dg.append(jnp.sum(p, axis=-1, keepdims=True) + jnp.exp(sink - mx))
                pg.append(p.astype(BF16))
            ps.append(jnp.concatenate(pg, axis=0))
            dens.append(dg)
        for u in range(ATT_UNROLL):
            o_all = _dot(ps[u], vbs[u])
            for g, sl in enumerate(heads(0)):
                o = o_all[g * CHUNK:(g + 1) * CHUNK] / dens[u][g]
                o_ref[pl.ds(q0s[u], CHUNK), sl] = o.astype(o_ref.dtype)
        return carry

    lax.fori_loop(0, seq // (CHUNK * ATT_UNROLL), blocks, 0)


def _attn_call(proj, slopes, sink):
    b, s, _ = proj.shape
    qw = ATT_GROUP * HEAD_DIM
    smem = pl.BlockSpec(memory_space=pltpu.SMEM)
    return pl.pallas_call(
        functools.partial(_attn_kernel, seq=s),
        out_shape=jax.ShapeDtypeStruct((N_BRANCH, b, s, BRANCH_W), BF16),
        grid=(b, ATT_KV_HEADS),
        in_specs=[smem, smem,
                  pl.BlockSpec((None, s, qw), lambda i, h: (i, 0, h)),
                  pl.BlockSpec((None, s, HEAD_DIM), lambda i, h: (i, 0, _OFF_AK + h)),
                  pl.BlockSpec((None, s, HEAD_DIM), lambda i, h: (i, 0, _OFF_AV + h))],
        out_specs=pl.BlockSpec((None, None, s, qw), lambda i, h: (0, i, 0, h)),
        compiler_params=_params(2),
        name="windowed_gqa",
    )(slopes, sink, proj, proj, proj)


def _ret_kernel(lg_ref, q_ref, k_ref, v_ref, g_ref, ybuf_ref, o_ref, sf_ref, *, seq):
    head = pl.program_id(1)
    lg = lg_ref[head]
    nc = seq // CHUNK
    c = CHUNK
    row = lax.broadcasted_iota(jnp.int32, (c, c), 0)
    col = lax.broadcasted_iota(jnp.int32, (c, c), 1)
    dmat = jnp.exp(lg * jnp.abs(row - col).astype(F32))
    pos = lax.broadcasted_iota(jnp.int32, (c, 1), 0).astype(F32)
    kdec_f = jnp.exp(lg * (c - 1.0 - pos))
    kdec_b = jnp.exp(lg * pos)
    qdec_f = jnp.exp(lg * (pos + 1.0))
    qdec_b = jnp.exp(lg * (c - pos))
    cdec = jnp.exp(lg * c)
    kscale = HEAD_DIM ** -0.5

    def fwd(n, st):
        r0 = pl.multiple_of(n * c, c)
        sf_ref[n] = st
        kc = k_ref[pl.ds(r0, c), :] * kscale
        vc = v_ref[pl.ds(r0, c), :].astype(BF16)
        return st * cdec + _dot_tn((kc * kdec_f).astype(BF16), vc)

    lax.fori_loop(0, nc, fwd, jnp.zeros((c, c), F32), unroll=RET_UNROLL)

    def bwd(i, st):
        n = nc - 1 - i
        r0 = pl.multiple_of(n * c, c)
        qc = q_ref[pl.ds(r0, c), :]
        kc = k_ref[pl.ds(r0, c), :] * kscale
        vc = v_ref[pl.ds(r0, c), :].astype(BF16)
        p = _dot_nt(qc.astype(BF16), kc.astype(BF16)) * dmat
        o = _dot(p.astype(BF16), vc)
        o = o + _dot((qc * qdec_f).astype(BF16), sf_ref[n].astype(BF16))
        o = o + _dot((qc * qdec_b).astype(BF16), st.astype(BF16))
        mu = jnp.mean(o, axis=-1, keepdims=True)
        oc = o - mu
        var = jnp.mean(oc * oc, axis=-1, keepdims=True)
        y = oc * lax.rsqrt(var + LN_EPS) * _silu(g_ref[pl.ds(r0, c), :])
        o_ref[pl.ds(r0, c), :] = y.astype(o_ref.dtype)
        return st * cdec + _dot_tn((kc * kdec_b).astype(BF16), vc)

    lax.fori_loop(0, nc, bwd, jnp.zeros((c, c), F32), unroll=RET_UNROLL)


_IN_PLACE = pl.BlockSpec(memory_space=pl.ANY)


def _ret_call(proj, log_gamma, ybuf):
    b, s, _ = proj.shape

    def col(off):
        return pl.BlockSpec((None, s, HEAD_DIM), lambda i, h: (i, 0, off + h))

    return pl.pallas_call(
        functools.partial(_ret_kernel, seq=s),
        out_shape=jax.ShapeDtypeStruct(ybuf.shape, ybuf.dtype),
        grid=(b, RET_HEADS),
        in_specs=[pl.BlockSpec(memory_space=pltpu.SMEM), col(_OFF_RQ), col(_OFF_RK), col(_OFF_RV), col(_OFF_RG),
                  _IN_PLACE],
        out_specs=pl.BlockSpec((None, None, s, HEAD_DIM), lambda i, h: (1, i, 0, h)),
        scratch_shapes=[pltpu.VMEM((s // CHUNK, CHUNK, CHUNK), F32)],
        input_output_aliases={5: 0},
        compiler_params=_params(2),
        name="retention",
    )(log_gamma, proj, proj, proj, proj, ybuf)


def _sgu_kernel(u0_ref, u1_ref, v0_ref, v1_ref, w_ref, b_ref, ybuf_ref, o_ref):
    half = u0_ref.shape[-1]
    v = jnp.concatenate([_gelu_tanh(v0_ref[...]), _gelu_tanh(v1_ref[...])], axis=-1)
    mu = jnp.mean(v, axis=-1, keepdims=True)
    vc = v - mu
    var = jnp.mean(vc * vc, axis=-1, keepdims=True)
    vn = (vc * lax.rsqrt(var + LN_EPS)).astype(BF16)
    nchunk = vn.shape[0] // CHUNK
    for g in range(SG_GROUPS):
        lo = g * HEAD_DIM
        vg = jnp.concatenate([vn[r * CHUNK:(r + 1) * CHUNK, lo:lo + HEAD_DIM] for r in range(nchunk)], axis=-1)
        mixed = _dot(w_ref[g], vg) + b_ref[:, g:g + 1]
        u_ref = u0_ref if lo < half else u1_ref
        ul = lo % half
        for r in range(nchunk):
            u = _gelu_tanh(u_ref[r * CHUNK:(r + 1) * CHUNK, ul:ul + HEAD_DIM])
            o_ref[r * CHUNK:(r + 1) * CHUNK, lo:lo + HEAD_DIM] = (
                u * mixed[:, r * HEAD_DIM:(r + 1) * HEAD_DIM]).astype(o_ref.dtype)


def _sgu_call(proj, sg_w, sg_bt, ybuf):
    b, s, _ = proj.shape
    half = SG_GROUPS * HEAD_DIM // 2
    hb = half // HEAD_DIM
    rows = _pick(s, (4 * CHUNK, 2 * CHUNK, CHUNK))

    def blk(off):
        return pl.BlockSpec((None, rows, half), lambda i, n: (i, n, off))

    return pl.pallas_call(
        _sgu_kernel,
        out_shape=jax.ShapeDtypeStruct(ybuf.shape, ybuf.dtype),
        grid=(b, s // rows),
        in_specs=[blk(_OFF_SU // hb), blk(_OFF_SU // hb + 1), blk(_OFF_SV // hb), blk(_OFF_SV // hb + 1),
                  pl.BlockSpec((SG_GROUPS, CHUNK, CHUNK), lambda i, n: (0, 0, 0)),
                  pl.BlockSpec((CHUNK, SG_GROUPS), lambda i, n: (0, 0)),
                  _IN_PLACE],
        out_specs=pl.BlockSpec((None, None, rows, BRANCH_W), lambda i, n: (2, i, n, 0)),
        input_output_aliases={6: 0},
        compiler_params=_params(2),
        name="spatial_gating",
    )(proj, proj, proj, proj, sg_w, sg_bt, ybuf)


def _split3(x):
    x1 = x.astype(BF16)
    r1 = x - x1.astype(F32)
    x2 = r1.astype(BF16)
    x3 = (r1 - x2.astype(F32)).astype(BF16)
    return x1, x2, x3


def _hgrn_tables(forward):
    c = CHUNK
    row = np.arange(c)[:, None]
    col = np.arange(c)[None, :]
    lmat = ((col <= row) if forward else (col >= row)).astype(np.float32)
    qms, lms, lsub = [], [], []
    b = c // 2
    while b >= SUB:
        upper_r = (row % (2 * b)) >= b
        upper_c = (col % (2 * b)) >= b
        q_side = upper_r if forward else ~upper_r
        k_side = ~upper_c if forward else upper_c
        pair = (q_side & k_side & ((row // (2 * b)) == (col // (2 * b)))).astype(np.float32)
        if b >= 8:
            lsub.append(pair[q_side[:, 0]])
        else:
            qms.append(np.broadcast_to(q_side, (c, HEAD_DIM)).astype(np.float32))
            lms.append(pair)
        b //= 2
    off = np.where((row // SUB) == (col // SUB), col - row, 2 * c).astype(np.int32)
    return lmat, np.stack(qms), np.stack(lms), off, np.stack(lsub)


def _hgrn_chunks(qzv, lb, st, tmat, qmask_ref, lmask_ref, lsub_ref, off3, forward):
    c = CHUNK
    tiles = (c // 8, 8, c)
    gates = []
    for q, z, v in qzv:
        f = lb + (1.0 - lb) * _sigmoid(z)
        g1, g2, g3 = _split3(jnp.log2(f))
        a = _dot(tmat, g1) + _dot(tmat, g2) + _dot(tmat, g3)
        gates.append((1.0 - f, a))

    pairs = []
    for (q, z, v), (kk, a) in zip(qzv, gates):
        p_tiles = [None] * (c // 8)

        def add_rows(first_tile, block):
            for i in range(block.shape[0] // 8):
                t = first_tile + i
                x = block[8 * i:8 * i + 8]
                p_tiles[t] = x if p_tiles[t] is None else p_tiles[t] + x

        n_sliced = 0
        for li in range(_HGRN_LEVELS):
            b = c >> (li + 1)
            nb = c // (2 * b)
            a4 = a.reshape(nb, 2 * b, c)
            mid = a4[:, b - 1:b, :] if forward else a4[:, b:b + 1, :]
            if b >= 8:
                qs, ks = (slice(b, 2 * b), slice(0, b)) if forward else (slice(0, b), slice(b, 2 * b))
                q4, k4 = q.reshape(nb, 2 * b, c), kk.reshape(nb, 2 * b, c)
                qt = q4[:, qs, :] * jnp.exp2(a4[:, qs, :] - mid)
                kt = k4[:, ks, :] * jnp.exp2(mid - a4[:, ks, :])
                zero = jnp.zeros_like(kt)
                kt = jnp.concatenate([kt, zero] if forward else [zero, kt], axis=1)
                ps = _dot_nt(qt.reshape(c // 2, c).astype(BF16), kt.reshape(c, c).astype(BF16)) * lsub_ref[n_sliced]
                n_sliced += 1
                for blk in range(nb):
                    add_rows((blk * 2 * b + qs.start) // 8, ps[blk * b:(blk + 1) * b])
            else:
                lm = li - n_sliced
                d = (a4 - mid).reshape(c, c)
                u = ((kk + qmask_ref[lm] * (q - kk)) * jnp.exp2(-jnp.abs(d))).astype(BF16)
                add_rows(0, _dot_nt(u, u) * lmask_ref[lm])

        a3, k3, q3 = a.reshape(tiles), kk.reshape(tiles), q.reshape(tiles)
        p3 = jnp.stack(p_tiles)
        for delta in range(SUB):
            if delta == 0:
                w = q3 * k3
            else:
                sh = delta if forward else 8 - delta
                a_s = pltpu.roll(a3, sh, 1)
                k_s = pltpu.roll(k3, sh, 1)
                w = q3 * k_s * jnp.exp2(jnp.minimum(a3 - a_s, 0.0))
            rs = jnp.sum(w, axis=-1, keepdims=True)
            p3 = jnp.where(off3 == (-delta if forward else delta), rs, p3)
        pairs.append(p3.reshape(c, c).astype(BF16))

    outs = []
    for (q, z, v), (kk, a), p in zip(qzv, gates, pairs):
        a_end = a[c - 1:c] if forward else a[0:1]
        vb = v.astype(BF16)
        o = _dot_nt((q * jnp.exp2(a)).astype(BF16), st.astype(BF16)) + _dot(p, vb)
        st = st * jnp.exp2(a_end) + _dot_tn(vb, (kk * jnp.exp2(a_end - a)).astype(BF16))
        outs.append(o)
    return outs, st


def _hgrn_kernel(q_ref, zf_ref, zb_ref, v_ref, g_ref, lb_ref, tmat_ref, qmask_ref, lmask_ref, lsub_ref, off_ref,
                 ybuf_ref, o_ref, of_ref, *, seq):
    nc = seq // CHUNK
    c = CHUNK
    lb = lb_ref[...]
    off3 = off_ref[...].reshape(c // 8, 8, c)

    def rows(n):
        return pl.ds(pl.multiple_of(n * c, c), c)

    def fwd(it, st):
        ns = [it * HGRN_UNROLL + u for u in range(HGRN_UNROLL)]
        qzv = [(q_ref[rows(n), :], zf_ref[rows(n), :], v_ref[rows(n), :]) for n in ns]
        outs, st = _hgrn_chunks(qzv, lb, st, tmat_ref[0], qmask_ref.at[0], lmask_ref.at[0], lsub_ref.at[0], off3, True)
        for n, o in zip(ns, outs):
            of_ref[rows(n), :] = o
        return st

    lax.fori_loop(0, nc // HGRN_UNROLL, fwd, jnp.zeros((c, c), F32))

    def bwd(it, st):
        ns = [nc - 1 - (it * HGRN_UNROLL + u) for u in range(HGRN_UNROLL)]
        qzv = [(q_ref[rows(n), :], zb_ref[rows(n), :], v_ref[rows(n), :]) for n in ns]
        outs, st = _hgrn_chunks(qzv, lb, st, tmat_ref[1], qmask_ref.at[1], lmask_ref.at[1], lsub_ref.at[1], off3,
                                False)
        for n, o in zip(ns, outs):
            o = o + of_ref[rows(n), :]
            y = o * lax.rsqrt(jnp.mean(o * o, axis=-1, keepdims=True) + LN_EPS) * _silu(g_ref[rows(n), :])
            o_ref[rows(n), :] = y.astype(o_ref.dtype)
        return st

    lax.fori_loop(0, nc // HGRN_UNROLL, bwd, jnp.zeros((c, c), F32))


def _hgrn_call(proj, lb, ybuf):
    b, s, _ = proj.shape
    tf, tb = _hgrn_tables(True), _hgrn_tables(False)
    tmat = jnp.asarray(np.stack([tf[0], tb[0]]), BF16)
    qmask = jnp.asarray(np.stack([tf[1], tb[1]]))
    lmask = jnp.asarray(np.stack([tf[2], tb[2]]))
    lsub = jnp.asarray(np.stack([tf[4], tb[4]]))
    off = jnp.asarray(tf[3])

    def col(off_):
        return pl.BlockSpec((None, s, HEAD_DIM), lambda i, h: (i, 0, off_ + h))

    def whole(arr):
        return pl.BlockSpec(arr.shape, lambda i, h: (0,) * arr.ndim)

    return pl.pallas_call(
        functools.partial(_hgrn_kernel, seq=s),
        out_shape=jax.ShapeDtypeStruct(ybuf.shape, ybuf.dtype),
        grid=(b, HGRN_HEADS),
        in_specs=[col(_OFF_DQ), col(_OFF_DFF), col(_OFF_DFB), col(_OFF_DI), col(_OFF_DG),
                  pl.BlockSpec((1, HEAD_DIM), lambda i, h: (0, h)),
                  whole(tmat), whole(qmask), whole(lmask), whole(lsub), whole(off), _IN_PLACE],
        out_specs=pl.BlockSpec((None, None, s, HEAD_DIM), lambda i, h: (3, i, 0, h)),
        scratch_shapes=[pltpu.VMEM((s, HEAD_DIM), F32)],
        input_output_aliases={11: 0},
        compiler_params=_params(2),
        name="hgrn2",
    )(proj, proj, proj, proj, proj, lb, tmat, qmask, lmask, lsub, off, ybuf)


def kernel(x, c, w_in, attn_sink, sg_w, sg_b, hgrn_lb_logits, w_branch, w_gate, w_o, w_mod, b_mod, ln_g, ln_b, w_ffn_in, w_ffn_out):
    bsz, seq, d = x.shape
    depth = w_in.shape[0]
    m = bsz * seq
    alpha = (2.0 * depth) ** 0.25

    pz = jax.nn.softmax(hgrn_lb_logits.astype(F32), axis=0)
    lower_bounds = jnp.cumsum(pz, axis=0) - pz[:1]
    slopes = jnp.exp2(-8.0 * jnp.arange(1, ATT_HEADS + 1, dtype=F32) / ATT_HEADS)
    log_gamma = jnp.log1p(-jnp.exp2(-5.0 - jnp.arange(RET_HEADS, dtype=F32)))

    rows = 8
    c_pad = jnp.zeros((rows, d), F32).at[:bsz].set(c)
    mod = _mod_call(c_pad, w_mod, b_mod)[:, :bsz]

    def mod_part(l, i):
        return mod[l, :, i * d:(i + 1) * d].reshape(bsz, 1, d)

    tm = _pick(m, (1024, 512, 256, 128))
    tn = _pick(d, (512, 256, 128))
    h = _modulate_call(x, mod_part(0, 1), mod_part(0, 0))
    for l in range(depth):
        h2d = h.reshape(m, d)
        proj = _matmul(h2d, w_in, l, tm=tm, tn=_pick(w_in.shape[2], (512, 256, 128)),
                       out_dtype=F32, name="in_proj").reshape(bsz, seq, -1)
        ys = _attn_call(proj, slopes, attn_sink[l].astype(F32))
        ys = _ret_call(proj, log_gamma, ys)
        ys = _sgu_call(proj, sg_w[l].astype(BF16), sg_b[l].T, ys)
        ys = _hgrn_call(proj, lower_bounds[l].reshape(1, -1), ys)
        merged = _merge_call(h2d, ys.reshape(N_BRANCH, m, BRANCH_W), w_gate, w_branch, l)
        mix = _matmul(merged, w_o, l, tm=tm, tn=tn, out_dtype=BF16, name="out_proj").reshape(bsz, seq, d)
        x, h = _ln_call(x, mix, mod_part(l, 2), ln_g[l, 0:1], ln_b[l, 0:1], alpha, mod_part(l, 4), mod_part(l, 3))
        act = _ffn_in_call(h.reshape(m, d), w_ffn_in, l, _pick(m, (2048, 1024, 512, 256, 128)))
        ffn = _matmul(act, w_ffn_out, l, tm=tm, tn=_pick(d, (256, 128)), out_dtype=BF16, name="ffn_out",
                      lhs_buffers=1).reshape(bsz, seq, d)
        if l + 1 < depth:
            x, h = _ln_call(x, ffn, mod_part(l, 5), ln_g[l, 1:2], ln_b[l, 1:2], alpha,
                            mod_part(l + 1, 1), mod_part(l + 1, 0))
        else:
            x, _ = _ln_call(x, ffn, mod_part(l, 5), ln_g[l, 1:2], ln_b[l, 1:2], alpha)
    return x
```

```python
import functools

import jax
import jax.numpy as jnp
import numpy as np
from jax import lax
from jax.experimental import pallas as pl
from jax.experimental.pallas import tpu as pltpu

F32 = jnp.float32
BF16 = jnp.bfloat16

HEAD_DIM = 128
ATT_HEADS = 8
ATT_KV_HEADS = 2
ATT_GROUP = ATT_HEADS // ATT_KV_HEADS
ATT_WINDOW = 128
RET_HEADS = 8
SG_GROUPS = 8
HGRN_HEADS = 8
CHUNK = 128
SUB = 4
_HGRN_LEVELS = (CHUNK // SUB).bit_length() - 1
HGRN_UNROLL = 8
RET_UNROLL = 8
ATT_UNROLL = 8
BRANCH_W = 8 * HEAD_DIM
N_BRANCH = 4
LN_EPS = 1e-5

_OFF_AQ = 0
_OFF_AK = _OFF_AQ + ATT_HEADS
_OFF_AV = _OFF_AK + ATT_KV_HEADS
_OFF_RQ = _OFF_AV + ATT_KV_HEADS
_OFF_RK = _OFF_RQ + RET_HEADS
_OFF_RV = _OFF_RK + RET_HEADS
_OFF_RG = _OFF_RV + RET_HEADS
_OFF_SU = _OFF_RG + RET_HEADS
_OFF_SV = _OFF_SU + SG_GROUPS
_OFF_DQ = _OFF_SV + SG_GROUPS
_OFF_DFF = _OFF_DQ + HGRN_HEADS
_OFF_DFB = _OFF_DFF + HGRN_HEADS
_OFF_DI = _OFF_DFB + HGRN_HEADS
_OFF_DG = _OFF_DI + HGRN_HEADS

VMEM_LIMIT_BYTES_V7X = 56 * 1024 * 1024


def _params(n_axes):
    return pltpu.CompilerParams(dimension_semantics=("arbitrary",) * n_axes,
                                vmem_limit_bytes=VMEM_LIMIT_BYTES_V7X)


def _dot(a, b):
    return jnp.dot(a, b, preferred_element_type=F32)


def _dot_nt(a, b):
    return lax.dot_general(a, b, (((1,), (1,)), ((), ())), preferred_element_type=F32)


def _dot_tn(a, b):
    return lax.dot_general(a, b, (((0,), (0,)), ((), ())), preferred_element_type=F32)


def _sigmoid(x):
    return 1.0 / (1.0 + jnp.exp(-x))


def _silu(x):
    return x * _sigmoid(x)


def _gelu_tanh(x):
    return 0.5 * x * (1.0 + jnp.tanh(0.7978845608028654 * (x + 0.044715 * (x * x * x))))


def _pick(n, prefs):
    for p in prefs:
        if n % p == 0:
            return p
    return n


def _mod_kernel(c_ref, w_ref, b_ref, o_ref):
    ca = _silu(c_ref[...]).astype(BF16)
    o_ref[...] = _dot(ca, w_ref[...].astype(BF16)) + b_ref[...]


def _mod_call(c_pad, w_mod, b_mod):
    depth, d, n = w_mod.shape
    rows = c_pad.shape[0]
    tn = _pick(n, (512, 256, 128))
    return pl.pallas_call(
        _mod_kernel,
        out_shape=jax.ShapeDtypeStruct((depth, rows, n), F32),
        grid=(depth, n // tn),
        in_specs=[pl.BlockSpec((rows, d), lambda l, j: (0, 0)),
                  pl.BlockSpec((None, d, tn), lambda l, j: (l, 0, j)),
                  pl.BlockSpec((None, 1, tn), lambda l, j: (l, 0, j))],
        out_specs=pl.BlockSpec((None, rows, tn), lambda l, j: (l, 0, j)),
        compiler_params=_params(2),
        name="adaln_mod",
    )(c_pad, w_mod, b_mod.reshape(depth, 1, n))


def _modulate_kernel(x_ref, sc_ref, sh_ref, o_ref):
    o_ref[...] = (x_ref[...] * (1.0 + sc_ref[...]) + sh_ref[...]).astype(o_ref.dtype)


def _modulate_call(x, sc, sh):
    b, s, d = x.shape
    ts = _pick(s, (512, 256, 128))
    return pl.pallas_call(
        _modulate_kernel,
        out_shape=jax.ShapeDtypeStruct((b, s, d), BF16),
        grid=(b, s // ts),
        in_specs=[pl.BlockSpec((None, ts, d), lambda i, j: (i, j, 0)),
                  pl.BlockSpec((None, 1, d), lambda i, j: (i, 0, 0)),
                  pl.BlockSpec((None, 1, d), lambda i, j: (i, 0, 0))],
        out_specs=pl.BlockSpec((None, ts, d), lambda i, j: (i, j, 0)),
        compiler_params=_params(2),
        name="modulate",
    )(x, sc, sh)


def _mm_kernel(x_ref, w_ref, o_ref):
    o_ref[...] = _dot(x_ref[...], w_ref[...].astype(BF16)).astype(o_ref.dtype)


def _matmul(x, w, layer, *, tm, tn, out_dtype, name, lhs_buffers=2):
    m, k = x.shape
    n = w.shape[2]
    return pl.pallas_call(
        _mm_kernel,
        out_shape=jax.ShapeDtypeStruct((m, n), out_dtype),
        grid=(m // tm, n // tn),
        in_specs=[pl.BlockSpec((tm, k), lambda i, j: (i, 0), pipeline_mode=pl.Buffered(lhs_buffers)),
                  pl.BlockSpec((None, k, tn), lambda i, j: (layer, 0, j))],
        out_specs=pl.BlockSpec((tm, tn), lambda i, j: (i, j)),
        compiler_params=_params(2),
        name=name,
    )(x, w)


def _merge_kernel(h_ref, ya_ref, yb_ref, yc_ref, yd_ref, wg_ref, wb_ref, o_ref, acc_ref, ys_ref):
    br = pl.program_id(2)

    @pl.when((pl.program_id(1) == 0) & (br == 0))
    def _():
        for i, y_ref in enumerate((ya_ref, yb_ref, yc_ref, yd_ref)):
            ys_ref[i] = y_ref[...]

    @pl.when(br == 0)
    def _():
        acc_ref[...] = jnp.zeros_like(acc_ref)

    gate = _sigmoid(_dot(h_ref[...], wg_ref[...].astype(BF16)))
    acc = acc_ref[...] + gate * _dot(ys_ref[br], wb_ref[...].astype(BF16))
    acc_ref[...] = acc
    o_ref[...] = acc.astype(o_ref.dtype)


def _merge_call(h, ys, wg, wb, layer):
    m, d = h.shape
    bw = ys[0].shape[1]
    n = wg.shape[3]
    tm = _pick(m, (1024, 512, 256, 128))
    tn = _pick(n, (512, 256, 128))
    y_spec = pl.BlockSpec((tm, bw), lambda i, j, r: (i, 0), pipeline_mode=pl.Buffered(1))
    return pl.pallas_call(
        _merge_kernel,
        out_shape=jax.ShapeDtypeStruct((m, n), BF16),
        grid=(m // tm, n // tn, N_BRANCH),
        in_specs=[pl.BlockSpec((tm, d), lambda i, j, r: (i, 0), pipeline_mode=pl.Buffered(1)),
                  y_spec, y_spec, y_spec, y_spec,
                  pl.BlockSpec((None, None, d, tn), lambda i, j, r: (layer, r, 0, j)),
                  pl.BlockSpec((None, None, bw, tn), lambda i, j, r: (layer, r, 0, j))],
        out_specs=pl.BlockSpec((tm, tn), lambda i, j, r: (i, j)),
        scratch_shapes=[pltpu.VMEM((tm, tn), F32), pltpu.VMEM((N_BRANCH, tm, bw), BF16)],
        compiler_params=_params(3),
        name="gated_merge",
    )(h, *ys, wg, wb)


def _residual_ln(x, m, gate, lg, lb, alpha):
    z = alpha * x + (1.0 + gate) * m
    mu = jnp.mean(z, axis=-1, keepdims=True)
    zc = z - mu
    var = jnp.mean(zc * zc, axis=-1, keepdims=True)
    return zc * lax.rsqrt(var + LN_EPS) * lg + lb


def _ln_mod_kernel(x_ref, m_ref, gate_ref, lg_ref, lb_ref, sc_ref, sh_ref, xo_ref, ho_ref, *, alpha):
    xn = _residual_ln(x_ref[...], m_ref[...], gate_ref[...], lg_ref[...], lb_ref[...], alpha)
    xo_ref[...] = xn
    ho_ref[...] = (xn * (1.0 + sc_ref[...]) + sh_ref[...]).astype(ho_ref.dtype)


def _ln_kernel(x_ref, m_ref, gate_ref, lg_ref, lb_ref, xo_ref, *, alpha):
    xo_ref[...] = _residual_ln(x_ref[...], m_ref[...], gate_ref[...], lg_ref[...], lb_ref[...], alpha)


def _ln_call(x, m, gate, ln_g, ln_b, alpha, sc=None, sh=None):
    b, s, d = x.shape
    ts = _pick(s, (256, 128))
    big = pl.BlockSpec((None, ts, d), lambda i, j: (i, j, 0))
    per_b = pl.BlockSpec((None, 1, d), lambda i, j: (i, 0, 0))
    shared = pl.BlockSpec((1, d), lambda i, j: (0, 0))
    x_shape = jax.ShapeDtypeStruct((b, s, d), F32)
    if sc is None:
        return pl.pallas_call(
            functools.partial(_ln_kernel, alpha=alpha),
            out_shape=x_shape,
            grid=(b, s // ts),
            in_specs=[big, big, per_b, shared, shared],
            out_specs=big,
            compiler_params=_params(2),
            name="residual_ln_last",
        )(x, m, gate, ln_g, ln_b), None
    return pl.pallas_call(
        functools.partial(_ln_mod_kernel, alpha=alpha),
        out_shape=(x_shape, jax.ShapeDtypeStruct((b, s, d), BF16)),
        grid=(b, s // ts),
        in_specs=[big, big, per_b, shared, shared, per_b, per_b],
        out_specs=(big, big),
        compiler_params=_params(2),
        name="residual_ln",
    )(x, m, gate, ln_g, ln_b, sc, sh)


def _ffn_in_kernel(h_ref, wa_ref, wg_ref, o_ref):
    h = h_ref[...]
    a = _dot(h, wa_ref[...].astype(BF16))
    g = _dot(h, wg_ref[...].astype(BF16))
    o_ref[...] = (_silu(a) * g).astype(o_ref.dtype)


def _ffn_in_call(h, w, layer, tm):
    m, d = h.shape
    f = w.shape[2] // 2
    tn = _pick(f, (256, 128))
    nj = f // tn
    return pl.pallas_call(
        _ffn_in_kernel,
        out_shape=jax.ShapeDtypeStruct((m, f), BF16),
        grid=(m // tm, nj),
        in_specs=[pl.BlockSpec((tm, d), lambda i, j: (i, 0), pipeline_mode=pl.Buffered(1)),
                  pl.BlockSpec((None, d, tn), lambda i, j: (layer, 0, j)),
                  pl.BlockSpec((None, d, tn), lambda i, j: (layer, 0, j + nj))],
        out_specs=pl.BlockSpec((tm, tn), lambda i, j: (i, j)),
        compiler_params=_params(2),
        name="swiglu_in",
    )(h, w, w)


def _attn_kernel(slope_ref, sink_ref, q_ref, k_ref, v_ref, o_ref, *, seq):
    hkv = pl.program_id(1)
    span = 3 * CHUNK
    scale = HEAD_DIM ** -0.5

    def heads(lo):
        return [slice(lo + g * HEAD_DIM, lo + (g + 1) * HEAD_DIM) for g in range(ATT_GROUP)]

    def blocks(it, carry):
        q0s, vbs, scs = [], [], []
        for u in range(ATT_UNROLL):
            q0 = pl.multiple_of((it * ATT_UNROLL + u) * CHUNK, CHUNK)
            ks = pl.multiple_of(jnp.clip(q0 - CHUNK, 0, seq - span), CHUNK)
            kb = k_ref[pl.ds(ks, span), :].astype(BF16)
            vbs.append(v_ref[pl.ds(ks, span), :].astype(BF16))
            qs = jnp.concatenate([q_ref[pl.ds(q0, CHUNK), sl] for sl in heads(0)], axis=0).astype(BF16)
            scs.append((_dot_nt(qs, kb) * scale, ks))
            q0s.append(q0)
        ps, dens = [], []
        for u in range(ATT_UNROLL):
            sc_all, ks = scs[u]
            qpos = q0s[u] + lax.broadcasted_iota(jnp.int32, (CHUNK, span), 0)
            kpos = ks + lax.broadcasted_iota(jnp.int32, (CHUNK, span), 1)
            dist = jnp.abs(qpos - kpos)
            valid = dist <= ATT_WINDOW
            distf = dist.astype(F32)
            pg, dg = [], []
            for g in range(ATT_GROUP):
                head = hkv * ATT_GROUP + g
                sc = sc_all[g * CHUNK:(g + 1) * CHUNK] - slope_ref[head] * distf
                sc = jnp.where(valid, sc, -jnp.inf)
                sink = sink_ref[head]
                mx = jnp.maximum(jnp.max(sc, axis=-1, keepdims=True), sink)
                p = jnp.exp(sc - mx)
                dg.append(jnp.sum(p, axis=-1, keepdims=True) + jnp.exp(sink - mx))
                pg.append(p.astype(BF16))
            ps.append(jnp.concatenate(pg, axis=0))
            dens.append(dg)
        for u in range(ATT_UNROLL):
            o_all = _dot(ps[u], vbs[u])
            for g, sl in enumerate(heads(0)):
                o = o_all[g * CHUNK:(g + 1) * CHUNK] / dens[u][g]
                o_ref[pl.ds(q0s[u], CHUNK), sl] = o.astype(o_ref.dtype)
        return carry

    lax.fori_loop(0, seq // (CHUNK * ATT_UNROLL), blocks, 0)


def _attn_call(proj, slopes, sink):
    b, s, _ = proj.shape
    assert s % (CHUNK * ATT_UNROLL) == 0 and s >= 3 * CHUNK, s
    qw = ATT_GROUP * HEAD_DIM
    smem = pl.BlockSpec(memory_space=pltpu.SMEM)
    return pl.pallas_call(
        functools.partial(_attn_kernel, seq=s),
        out_shape=jax.ShapeDtypeStruct((b, s, BRANCH_W), BF16),
        grid=(b, ATT_KV_HEADS),
        in_specs=[smem, smem,
                  pl.BlockSpec((None, s, qw), lambda i, h: (i, 0, h)),
                  pl.BlockSpec((None, s, HEAD_DIM), lambda i, h: (i, 0, _OFF_AK + h)),
                  pl.BlockSpec((None, s, HEAD_DIM), lambda i, h: (i, 0, _OFF_AV + h))],
        out_specs=pl.BlockSpec((None, s, qw), lambda i, h: (i, 0, h)),
        compiler_params=_params(2),
        name="windowed_gqa",
    )(slopes, sink, proj, proj, proj)


def _ret_kernel(lg_ref, q_ref, k_ref, v_ref, g_ref, o_ref, sf_ref, *, seq):
    head = pl.program_id(1)
    lg = lg_ref[head]
    nc = seq // CHUNK
    c = CHUNK
    row = lax.broadcasted_iota(jnp.int32, (c, c), 0)
    col = lax.broadcasted_iota(jnp.int32, (c, c), 1)
    dmat = jnp.exp(lg * jnp.abs(row - col).astype(F32))
    pos = lax.broadcasted_iota(jnp.int32, (c, 1), 0).astype(F32)
    kdec_f = jnp.exp(lg * (c - 1.0 - pos))
    kdec_b = jnp.exp(lg * pos)
    qdec_f = jnp.exp(lg * (pos + 1.0))
    qdec_b = jnp.exp(lg * (c - pos))
    cdec = jnp.exp(lg * c)
    kscale = HEAD_DIM ** -0.5

    def fwd(n, st):
        r0 = pl.multiple_of(n * c, c)
        sf_ref[n] = st
        kc = k_ref[pl.ds(r0, c), :] * kscale
        vc = v_ref[pl.ds(r0, c), :].astype(BF16)
        return st * cdec + _dot_tn((kc * kdec_f).astype(BF16), vc)

    lax.fori_loop(0, nc, fwd, jnp.zeros((c, c), F32), unroll=RET_UNROLL)

    def bwd(i, st):
        n = nc - 1 - i
        r0 = pl.multiple_of(n * c, c)
        qc = q_ref[pl.ds(r0, c), :]
        kc = k_ref[pl.ds(r0, c), :] * kscale
        vc = v_ref[pl.ds(r0, c), :].astype(BF16)
        p = _dot_nt(qc.astype(BF16), kc.astype(BF16)) * dmat
        o = _dot(p.astype(BF16), vc)
        o = o + _dot((qc * qdec_f).astype(BF16), sf_ref[n].astype(BF16))
        o = o + _dot((qc * qdec_b).astype(BF16), st.astype(BF16))
        mu = jnp.mean(o, axis=-1, keepdims=True)
        oc = o - mu
        var = jnp.mean(oc * oc, axis=-1, keepdims=True)
        y = oc * lax.rsqrt(var + LN_EPS) * _silu(g_ref[pl.ds(r0, c), :])
        o_ref[pl.ds(r0, c), :] = y.astype(o_ref.dtype)
        return st * cdec + _dot_tn((kc * kdec_b).astype(BF16), vc)

    lax.fori_loop(0, nc, bwd, jnp.zeros((c, c), F32), unroll=RET_UNROLL)


def _ret_call(proj, log_gamma):
    b, s, _ = proj.shape

    def col(off):
        return pl.BlockSpec((None, s, HEAD_DIM), lambda i, h: (i, 0, off + h))

    return pl.pallas_call(
        functools.partial(_ret_kernel, seq=s),
        out_shape=jax.ShapeDtypeStruct((b, s, BRANCH_W), BF16),
        grid=(b, RET_HEADS),
        in_specs=[pl.BlockSpec(memory_space=pltpu.SMEM), col(_OFF_RQ), col(_OFF_RK), col(_OFF_RV), col(_OFF_RG)],
        out_specs=pl.BlockSpec((None, s, HEAD_DIM), lambda i, h: (i, 0, h)),
        scratch_shapes=[pltpu.VMEM((s // CHUNK, CHUNK, CHUNK), F32)],
        compiler_params=_params(2),
        name="retention",
    )(log_gamma, proj, proj, proj, proj)


def _sgu_kernel(u0_ref, u1_ref, v0_ref, v1_ref, w_ref, b_ref, o_ref):
    half = u0_ref.shape[-1]
    v = jnp.concatenate([_gelu_tanh(v0_ref[...]), _gelu_tanh(v1_ref[...])], axis=-1)
    mu = jnp.mean(v, axis=-1, keepdims=True)
    vc = v - mu
    var = jnp.mean(vc * vc, axis=-1, keepdims=True)
    vn = (vc * lax.rsqrt(var + LN_EPS)).astype(BF16)
    nchunk = vn.shape[0] // CHUNK
    for g in range(SG_GROUPS):
        lo = g * HEAD_DIM
        vg = jnp.concatenate([vn[r * CHUNK:(r + 1) * CHUNK, lo:lo + HEAD_DIM] for r in range(nchunk)], axis=-1)
        mixed = _dot(w_ref[g], vg) + b_ref[:, g:g + 1]
        u_ref = u0_ref if lo < half else u1_ref
        ul = lo % half
        for r in range(nchunk):
            u = _gelu_tanh(u_ref[r * CHUNK:(r + 1) * CHUNK, ul:ul + HEAD_DIM])
            o_ref[r * CHUNK:(r + 1) * CHUNK, lo:lo + HEAD_DIM] = (
                u * mixed[:, r * HEAD_DIM:(r + 1) * HEAD_DIM]).astype(o_ref.dtype)


def _sgu_call(proj, sg_w, sg_bt):
    b, s, _ = proj.shape
    half = SG_GROUPS * HEAD_DIM // 2
    hb = half // HEAD_DIM
    rows = _pick(s, (4 * CHUNK, 2 * CHUNK, CHUNK))

    def blk(off):
        return pl.BlockSpec((None, rows, half), lambda i, n: (i, n, off))

    return pl.pallas_call(
        _sgu_kernel,
        out_shape=jax.ShapeDtypeStruct((b, s, BRANCH_W), BF16),
        grid=(b, s // rows),
        in_specs=[blk(_OFF_SU // hb), blk(_OFF_SU // hb + 1), blk(_OFF_SV // hb), blk(_OFF_SV // hb + 1),
                  pl.BlockSpec((SG_GROUPS, CHUNK, CHUNK), lambda i, n: (0, 0, 0)),
                  pl.BlockSpec((CHUNK, SG_GROUPS), lambda i, n: (0, 0))],
        out_specs=pl.BlockSpec((None, rows, BRANCH_W), lambda i, n: (i, n, 0)),
        compiler_params=_params(2),
        name="spatial_gating",
    )(proj, proj, proj, proj, sg_w, sg_bt)


def _split3(x):
    x1 = x.astype(BF16)
    r1 = x - x1.astype(F32)
    x2 = r1.astype(BF16)
    x3 = (r1 - x2.astype(F32)).astype(BF16)
    return x1, x2, x3


def _hgrn_tables(forward):
    c = CHUNK
    row = np.arange(c)[:, None]
    col = np.arange(c)[None, :]
    lmat = ((col <= row) if forward else (col >= row)).astype(np.float32)
    qms, lms, lsub = [], [], []
    b = c // 2
    while b >= SUB:
        upper_r = (row % (2 * b)) >= b
        upper_c = (col % (2 * b)) >= b
        q_side = upper_r if forward else ~upper_r
        k_side = ~upper_c if forward else upper_c
        pair = (q_side & k_side & ((row // (2 * b)) == (col // (2 * b)))).astype(np.float32)
        if b >= 8:
            lsub.append(pair[q_side[:, 0]])
        else:
            qms.append(np.broadcast_to(q_side, (c, HEAD_DIM)).astype(np.float32))
            lms.append(pair)
        b //= 2
    off = np.where((row // SUB) == (col // SUB), col - row, 2 * c).astype(np.int32)
    return lmat, np.stack(qms), np.stack(lms), off, np.stack(lsub)


def _hgrn_chunks(qzv, lb, st, tmat, qmask_ref, lmask_ref, lsub_ref, off3, forward):
    c = CHUNK
    tiles = (c // 8, 8, c)
    gates = []
    for q, z, v in qzv:
        f = lb + (1.0 - lb) * _sigmoid(z)
        g1, g2, g3 = _split3(jnp.log2(f))
        a = _dot(tmat, g1) + _dot(tmat, g2) + _dot(tmat, g3)
        gates.append((1.0 - f, a))

    pairs = []
    for (q, z, v), (kk, a) in zip(qzv, gates):
        p_tiles = [None] * (c // 8)

        def add_rows(first_tile, block):
            for i in range(block.shape[0] // 8):
                t = first_tile + i
                x = block[8 * i:8 * i + 8]
                p_tiles[t] = x if p_tiles[t] is None else p_tiles[t] + x

        n_sliced = 0
        for li in range(_HGRN_LEVELS):
            b = c >> (li + 1)
            nb = c // (2 * b)
            a4 = a.reshape(nb, 2 * b, c)
            mid = a4[:, b - 1:b, :] if forward else a4[:, b:b + 1, :]
            if b >= 8:
                qs, ks = (slice(b, 2 * b), slice(0, b)) if forward else (slice(0, b), slice(b, 2 * b))
                q4, k4 = q.reshape(nb, 2 * b, c), kk.reshape(nb, 2 * b, c)
                qt = q4[:, qs, :] * jnp.exp2(a4[:, qs, :] - mid)
                kt = k4[:, ks, :] * jnp.exp2(mid - a4[:, ks, :])
                zero = jnp.zeros_like(kt)
                kt = jnp.concatenate([kt, zero] if forward else [zero, kt], axis=1)
                ps = _dot_nt(qt.reshape(c // 2, c).astype(BF16), kt.reshape(c, c).astype(BF16)) * lsub_ref[n_sliced]
                n_sliced += 1
                for blk in range(nb):
                    add_rows((blk * 2 * b + qs.start) // 8, ps[blk * b:(blk + 1) * b])
            else:
                lm = li - n_sliced
                d = (a4 - mid).reshape(c, c)
                u = ((kk + qmask_ref[lm] * (q - kk)) * jnp.exp2(-jnp.abs(d))).astype(BF16)
                add_rows(0, _dot_nt(u, u) * lmask_ref[lm])

        a3, k3, q3 = a.reshape(tiles), kk.reshape(tiles), q.reshape(tiles)
        p3 = jnp.stack(p_tiles)
        for delta in range(SUB):
            if delta == 0:
                w = q3 * k3
            else:
                sh = delta if forward else 8 - delta
                a_s = pltpu.roll(a3, sh, 1)
                k_s = pltpu.roll(k3, sh, 1)
                w = q3 * k_s * jnp.exp2(jnp.minimum(a3 - a_s, 0.0))
            rs = jnp.sum(w, axis=-1, keepdims=True)
            p3 = jnp.where(off3 == (-delta if forward else delta), rs, p3)
        pairs.append(p3.reshape(c, c).astype(BF16))

    outs = []
    for (q, z, v), (kk, a), p in zip(qzv, gates, pairs):
        a_end = a[c - 1:c] if forward else a[0:1]
        vb = v.astype(BF16)
        o = _dot_nt((q * jnp.exp2(a)).astype(BF16), st.astype(BF16)) + _dot(p, vb)
        st = st * jnp.exp2(a_end) + _dot_tn(vb, (kk * jnp.exp2(a_end - a)).astype(BF16))
        outs.append(o)
    return outs, st


def _hgrn_kernel(q_ref, zf_ref, zb_ref, v_ref, g_ref, lb_ref, tmat_ref, qmask_ref, lmask_ref, lsub_ref, off_ref,
                 o_ref, of_ref, *, seq):
    nc = seq // CHUNK
    c = CHUNK
    lb = lb_ref[...]
    off3 = off_ref[...].reshape(c // 8, 8, c)

    def rows(n):
        return pl.ds(pl.multiple_of(n * c, c), c)

    def fwd(it, st):
        ns = [it * HGRN_UNROLL + u for u in range(HGRN_UNROLL)]
        qzv = [(q_ref[rows(n), :], zf_ref[rows(n), :], v_ref[rows(n), :]) for n in ns]
        outs, st = _hgrn_chunks(qzv, lb, st, tmat_ref[0], qmask_ref.at[0], lmask_ref.at[0], lsub_ref.at[0], off3, True)
        for n, o in zip(ns, outs):
            of_ref[rows(n), :] = o
        return st

    lax.fori_loop(0, nc // HGRN_UNROLL, fwd, jnp.zeros((c, c), F32))

    def bwd(it, st):
        ns = [nc - 1 - (it * HGRN_UNROLL + u) for u in range(HGRN_UNROLL)]
        qzv = [(q_ref[rows(n), :], zb_ref[rows(n), :], v_ref[rows(n), :]) for n in ns]
        outs, st = _hgrn_chunks(qzv, lb, st, tmat_ref[1], qmask_ref.at[1], lmask_ref.at[1], lsub_ref.at[1], off3,
                                False)
        for n, o in zip(ns, outs):
            o = o + of_ref[rows(n), :]
            y = o * lax.rsqrt(jnp.mean(o * o, axis=-1, keepdims=True) + LN_EPS) * _silu(g_ref[rows(n), :])
            o_ref[rows(n), :] = y.astype(o_ref.dtype)
        return st

    lax.fori_loop(0, nc // HGRN_UNROLL, bwd, jnp.zeros((c, c), F32))


def _hgrn_call(proj, lb):
    b, s, _ = proj.shape
    assert s % (CHUNK * HGRN_UNROLL) == 0, s
    tf, tb = _hgrn_tables(True), _hgrn_tables(False)
    tmat = jnp.asarray(np.stack([tf[0], tb[0]]), BF16)
    qmask = jnp.asarray(np.stack([tf[1], tb[1]]))
    lmask = jnp.asarray(np.stack([tf[2], tb[2]]))
    lsub = jnp.asarray(np.stack([tf[4], tb[4]]))
    off = jnp.asarray(tf[3])

    def col(off_):
        return pl.BlockSpec((None, s, HEAD_DIM), lambda i, h: (i, 0, off_ + h))

    def whole(arr):
        return pl.BlockSpec(arr.shape, lambda i, h: (0,) * arr.ndim)

    return pl.pallas_call(
        functools.partial(_hgrn_kernel, seq=s),
        out_shape=jax.ShapeDtypeStruct((b, s, BRANCH_W), BF16),
        grid=(b, HGRN_HEADS),
        in_specs=[col(_OFF_DQ), col(_OFF_DFF), col(_OFF_DFB), col(_OFF_DI), col(_OFF_DG),
                  pl.BlockSpec((1, HEAD_DIM), lambda i, h: (0, h)),
                  whole(tmat), whole(qmask), whole(lmask), whole(lsub), whole(off)],
        out_specs=pl.BlockSpec((None, s, HEAD_DIM), lambda i, h: (i, 0, h)),
        scratch_shapes=[pltpu.VMEM((s, HEAD_DIM), F32)],
        compiler_params=_params(2),
        name="hgrn2",
    )(proj, proj, proj, proj, proj, lb, tmat, qmask, lmask, lsub, off)


def kernel(x, c, w_in, attn_sink, sg_w, sg_b, hgrn_lb_logits, w_branch, w_gate, w_o, w_mod, b_mod, ln_g, ln_b, w_ffn_in, w_ffn_out):
    bsz, seq, d = x.shape
    depth = w_in.shape[0]
    m = bsz * seq
    alpha = (2.0 * depth) ** 0.25

    pz = jax.nn.softmax(hgrn_lb_logits.astype(F32), axis=0)
    lower_bounds = jnp.cumsum(pz, axis=0) - pz[:1]
    slopes = jnp.exp2(-8.0 * jnp.arange(1, ATT_HEADS + 1, dtype=F32) / ATT_HEADS)
    log_gamma = jnp.log1p(-jnp.exp2(-5.0 - jnp.arange(RET_HEADS, dtype=F32)))

    rows = 8
    c_pad = jnp.zeros((rows, d), F32).at[:bsz].set(c)
    mod = _mod_call(c_pad, w_mod, b_mod)[:, :bsz]

    def mod_part(l, i):
        return mod[l, :, i * d:(i + 1) * d].reshape(bsz, 1, d)

    tm = _pick(m, (1024, 512, 256, 128))
    tn = _pick(d, (512, 256, 128))
    h = _modulate_call(x, mod_part(0, 1), mod_part(0, 0))
    for l in range(depth):
        h2d = h.reshape(m, d)
        proj = _matmul(h2d, w_in, l, tm=tm, tn=_pick(w_in.shape[2], (512, 256, 128)),
                       out_dtype=F32, name="in_proj").reshape(bsz, seq, -1)
        ys = (_attn_call(proj, slopes, attn_sink[l].astype(F32)),
              _ret_call(proj, log_gamma),
              _sgu_call(proj, sg_w[l].astype(BF16), sg_b[l].T),
              _hgrn_call(proj, lower_bounds[l].reshape(1, -1)))
        merged = _merge_call(h2d, [y.reshape(m, BRANCH_W) for y in ys], w_gate, w_branch, l)
        mix = _matmul(merged, w_o, l, tm=tm, tn=tn, out_dtype=BF16, name="out_proj").reshape(bsz, seq, d)
        x, h = _ln_call(x, mix, mod_part(l, 2), ln_g[l, 0:1], ln_b[l, 0:1], alpha, mod_part(l, 4), mod_part(l, 3))
        act = _ffn_in_call(h.reshape(m, d), w_ffn_in, l, _pick(m, (2048, 1024, 512, 256, 128)))
        ffn = _matmul(act, w_ffn_out, l, tm=tm, tn=_pick(d, (256, 128)), out_dtype=BF16, name="ffn_out",
                      lhs_buffers=1).reshape(bsz, seq, d)
        if l + 1 < depth:
            x, h = _ln_call(x, ffn, mod_part(l, 5), ln_g[l, 1:2], ln_b[l, 1:2], alpha,
                            mod_part(l + 1, 1), mod_part(l + 1, 0))
        else:
            x, _ = _ln_call(x, ffn, mod_part(l, 5), ln_g[l, 1:2], ln_b[l, 1:2], alpha)
    return x
```

```python
import functools

import jax
import jax.numpy as jnp
import numpy as np
from jax import lax
from jax.experimental import pallas as pl
from jax.experimental.pallas import tpu as pltpu

F32 = jnp.float32
BF16 = jnp.bfloat16

HEAD_DIM = 128
ATT_HEADS = 8
ATT_KV_HEADS = 2
ATT_GROUP = ATT_HEADS // ATT_KV_HEADS
ATT_WINDOW = 128
RET_HEADS = 8
SG_GROUPS = 8
HGRN_HEADS = 8
CHUNK = 128
SUB = 4
_HGRN_LEVELS = (CHUNK // SUB).bit_length() - 1
HGRN_UNROLL = 8
RET_UNROLL = 8
ATT_UNROLL = 8
BRANCH_W = 8 * HEAD_DIM
N_BRANCH = 4
LN_EPS = 1e-5

_OFF_AQ = 0
_OFF_AK = _OFF_AQ + ATT_HEADS
_OFF_AV = _OFF_AK + ATT_KV_HEADS
_OFF_RQ = _OFF_AV + ATT_KV_HEADS
_OFF_RK = _OFF_RQ + RET_HEADS
_OFF_RV = _OFF_RK + RET_HEADS
_OFF_RG = _OFF_RV + RET_HEADS
_OFF_SU = _OFF_RG + RET_HEADS
_OFF_SV = _OFF_SU + SG_GROUPS
_OFF_DQ = _OFF_SV + SG_GROUPS
_OFF_DFF = _OFF_DQ + HGRN_HEADS
_OFF_DFB = _OFF_DFF + HGRN_HEADS
_OFF_DI = _OFF_DFB + HGRN_HEADS
_OFF_DG = _OFF_DI + HGRN_HEADS

VMEM_LIMIT_BYTES_V7X = 56 * 1024 * 1024


def _params(n_axes):
    return pltpu.CompilerParams(dimension_semantics=("arbitrary",) * n_axes,
                                vmem_limit_bytes=VMEM_LIMIT_BYTES_V7X)


def _dot(a, b):
    return jnp.dot(a, b, preferred_element_type=F32)


def _dot_nt(a, b):
    return lax.dot_general(a, b, (((1,), (1,)), ((), ())), preferred_element_type=F32)


def _dot_tn(a, b):
    return lax.dot_general(a, b, (((0,), (0,)), ((), ())), preferred_element_type=F32)


def _sigmoid(x):
    return 1.0 / (1.0 + jnp.exp(-x))


def _silu(x):
    return x * _sigmoid(x)


def _gelu_tanh(x):
    return 0.5 * x * (1.0 + jnp.tanh(0.7978845608028654 * (x + 0.044715 * (x * x * x))))


def _pick(n, prefs):
    for p in prefs:
        if n % p == 0:
            return p
    return n


def _mod_kernel(c_ref, w_ref, b_ref, o_ref):
    ca = _silu(c_ref[...]).astype(BF16)
    o_ref[...] = _dot(ca, w_ref[...].astype(BF16)) + b_ref[...]


def _mod_call(c_pad, w_mod, b_mod):
    depth, d, n = w_mod.shape
    rows = c_pad.shape[0]
    tn = _pick(n, (512, 256, 128))
    return pl.pallas_call(
        _mod_kernel,
        out_shape=jax.ShapeDtypeStruct((depth, rows, n), F32),
        grid=(depth, n // tn),
        in_specs=[pl.BlockSpec((rows, d), lambda l, j: (0, 0)),
                  pl.BlockSpec((None, d, tn), lambda l, j: (l, 0, j)),
                  pl.BlockSpec((None, 1, tn), lambda l, j: (l, 0, j))],
        out_specs=pl.BlockSpec((None, rows, tn), lambda l, j: (l, 0, j)),
        compiler_params=_params(2),
        name="adaln_mod",
    )(c_pad, w_mod, b_mod.reshape(depth, 1, n))


def _modulate_kernel(x_ref, sc_ref, sh_ref, o_ref):
    o_ref[...] = (x_ref[...] * (1.0 + sc_ref[...]) + sh_ref[...]).astype(o_ref.dtype)


def _modulate_call(x, sc, sh):
    b, s, d = x.shape
    ts = _pick(s, (512, 256, 128))
    return pl.pallas_call(
        _modulate_kernel,
        out_shape=jax.ShapeDtypeStruct((b, s, d), BF16),
        grid=(b, s // ts),
        in_specs=[pl.BlockSpec((None, ts, d), lambda i, j: (i, j, 0)),
                  pl.BlockSpec((None, 1, d), lambda i, j: (i, 0, 0)),
                  pl.BlockSpec((None, 1, d), lambda i, j: (i, 0, 0))],
        out_specs=pl.BlockSpec((None, ts, d), lambda i, j: (i, j, 0)),
        compiler_params=_params(2),
        name="modulate",
    )(x, sc, sh)


def _mm_kernel(x_ref, w_ref, o_ref):
    o_ref[...] = _dot(x_ref[...], w_ref[...].astype(BF16)).astype(o_ref.dtype)


def _matmul(x, w, layer, *, tm, tn, out_dtype, name, lhs_buffers=2):
    m, k = x.shape
    n = w.shape[2]
    return pl.pallas_call(
        _mm_kernel,
        out_shape=jax.ShapeDtypeStruct((m, n), out_dtype),
        grid=(m // tm, n // tn),
        in_specs=[pl.BlockSpec((tm, k), lambda i, j: (i, 0), pipeline_mode=pl.Buffered(lhs_buffers)),
                  pl.BlockSpec((None, k, tn), lambda i, j: (layer, 0, j))],
        out_specs=pl.BlockSpec((tm, tn), lambda i, j: (i, j)),
        compiler_params=_params(2),
        name=name,
    )(x, w)


def _merge_kernel(h_ref, y_ref, wg_ref, wb_ref, o_ref, acc_ref):
    @pl.when(pl.program_id(2) == 0)
    def _():
        acc_ref[...] = jnp.zeros_like(acc_ref)

    gate = _sigmoid(_dot(h_ref[...], wg_ref[...].astype(BF16)))
    acc = acc_ref[...] + gate * _dot(y_ref[...], wb_ref[...].astype(BF16))
    acc_ref[...] = acc
    o_ref[...] = acc.astype(o_ref.dtype)


def _merge_call(h, ys, wg, wb, layer):
    m, d = h.shape
    nb, _, bw = ys.shape
    n = wg.shape[3]
    tm = _pick(m, (1024, 512, 256, 128))
    tn = _pick(n, (512, 256, 128))
    return pl.pallas_call(
        _merge_kernel,
        out_shape=jax.ShapeDtypeStruct((m, n), BF16),
        grid=(m // tm, n // tn, nb),
        in_specs=[pl.BlockSpec((tm, d), lambda i, j, r: (i, 0), pipeline_mode=pl.Buffered(1)),
                  pl.BlockSpec((None, tm, bw), lambda i, j, r: (r, i, 0)),
                  pl.BlockSpec((None, None, d, tn), lambda i, j, r: (layer, r, 0, j)),
                  pl.BlockSpec((None, None, bw, tn), lambda i, j, r: (layer, r, 0, j))],
        out_specs=pl.BlockSpec((tm, tn), lambda i, j, r: (i, j)),
        scratch_shapes=[pltpu.VMEM((tm, tn), F32)],
        compiler_params=_params(3),
        name="gated_merge",
    )(h, ys, wg, wb)


def _residual_ln(x, m, gate, lg, lb, alpha):
    z = alpha * x + (1.0 + gate) * m
    mu = jnp.mean(z, axis=-1, keepdims=True)
    zc = z - mu
    var = jnp.mean(zc * zc, axis=-1, keepdims=True)
    return zc * lax.rsqrt(var + LN_EPS) * lg + lb


def _ln_mod_kernel(x_ref, m_ref, gate_ref, lg_ref, lb_ref, sc_ref, sh_ref, xo_ref, ho_ref, *, alpha):
    xn = _residual_ln(x_ref[...], m_ref[...], gate_ref[...], lg_ref[...], lb_ref[...], alpha)
    xo_ref[...] = xn
    ho_ref[...] = (xn * (1.0 + sc_ref[...]) + sh_ref[...]).astype(ho_ref.dtype)


def _ln_kernel(x_ref, m_ref, gate_ref, lg_ref, lb_ref, xo_ref, *, alpha):
    xo_ref[...] = _residual_ln(x_ref[...], m_ref[...], gate_ref[...], lg_ref[...], lb_ref[...], alpha)


def _ln_call(x, m, gate, ln_g, ln_b, alpha, sc=None, sh=None):
    b, s, d = x.shape
    ts = _pick(s, (256, 128))
    big = pl.BlockSpec((None, ts, d), lambda i, j: (i, j, 0))
    per_b = pl.BlockSpec((None, 1, d), lambda i, j: (i, 0, 0))
    shared = pl.BlockSpec((1, d), lambda i, j: (0, 0))
    x_shape = jax.ShapeDtypeStruct((b, s, d), F32)
    if sc is None:
        return pl.pallas_call(
            functools.partial(_ln_kernel, alpha=alpha),
            out_shape=x_shape,
            grid=(b, s // ts),
            in_specs=[big, big, per_b, shared, shared],
            out_specs=big,
            compiler_params=_params(2),
            name="residual_ln_last",
        )(x, m, gate, ln_g, ln_b), None
    return pl.pallas_call(
        functools.partial(_ln_mod_kernel, alpha=alpha),
        out_shape=(x_shape, jax.ShapeDtypeStruct((b, s, d), BF16)),
        grid=(b, s // ts),
        in_specs=[big, big, per_b, shared, shared, per_b, per_b],
        out_specs=(big, big),
        compiler_params=_params(2),
        name="residual_ln",
    )(x, m, gate, ln_g, ln_b, sc, sh)


def _ffn_in_kernel(h_ref, wa_ref, wg_ref, o_ref):
    h = h_ref[...]
    a = _dot(h, wa_ref[...].astype(BF16))
    g = _dot(h, wg_ref[...].astype(BF16))
    o_ref[...] = (_silu(a) * g).astype(o_ref.dtype)


def _ffn_in_call(h, w, layer, tm):
    m, d = h.shape
    f = w.shape[2] // 2
    tn = _pick(f, (256, 128))
    nj = f // tn
    return pl.pallas_call(
        _ffn_in_kernel,
        out_shape=jax.ShapeDtypeStruct((m, f), BF16),
        grid=(m // tm, nj),
        in_specs=[pl.BlockSpec((tm, d), lambda i, j: (i, 0), pipeline_mode=pl.Buffered(1)),
                  pl.BlockSpec((None, d, tn), lambda i, j: (layer, 0, j)),
                  pl.BlockSpec((None, d, tn), lambda i, j: (layer, 0, j + nj))],
        out_specs=pl.BlockSpec((tm, tn), lambda i, j: (i, j)),
        compiler_params=_params(2),
        name="swiglu_in",
    )(h, w, w)


def _attn_kernel(slope_ref, sink_ref, q_ref, k_ref, v_ref, ybuf_ref, o_ref, *, seq):
    hkv = pl.program_id(1)
    span = 3 * CHUNK
    scale = HEAD_DIM ** -0.5

    def heads(lo):
        return [slice(lo + g * HEAD_DIM, lo + (g + 1) * HEAD_DIM) for g in range(ATT_GROUP)]

    def blocks(it, carry):
        q0s, vbs, scs = [], [], []
        for u in range(ATT_UNROLL):
            q0 = pl.multiple_of((it * ATT_UNROLL + u) * CHUNK, CHUNK)
            ks = pl.multiple_of(jnp.clip(q0 - CHUNK, 0, seq - span), CHUNK)
            kb = k_ref[pl.ds(ks, span), :].astype(BF16)
            vbs.append(v_ref[pl.ds(ks, span), :].astype(BF16))
            qs = jnp.concatenate([q_ref[pl.ds(q0, CHUNK), sl] for sl in heads(0)], axis=0).astype(BF16)
            scs.append((_dot_nt(qs, kb) * scale, ks))
            q0s.append(q0)
        ps, dens = [], []
        for u in range(ATT_UNROLL):
            sc_all, ks = scs[u]
            qpos = q0s[u] + lax.broadcasted_iota(jnp.int32, (CHUNK, span), 0)
            kpos = ks + lax.broadcasted_iota(jnp.int32, (CHUNK, span), 1)
            dist = jnp.abs(qpos - kpos)
            valid = dist <= ATT_WINDOW
            distf = dist.astype(F32)
            pg, dg = [], []
            for g in range(ATT_GROUP):
                head = hkv * ATT_GROUP + g
                sc = sc_all[g * CHUNK:(g + 1) * CHUNK] - slope_ref[head] * distf
                sc = jnp.where(valid, sc, -jnp.inf)
                sink = sink_ref[head]
                mx = jnp.maximum(jnp.max(sc, axis=-1, keepdims=True), sink)
                p = jnp.exp(sc - mx)
                dg.append(jnp.sum(p, axis=-1, keepdims=True) + jnp.exp(sink - mx))
                pg.append(p.astype(BF16))
            ps.append(jnp.concatenate(pg, axis=0))
            dens.append(dg)
        for u in range(ATT_UNROLL):
            o_all = _dot(ps[u], vbs[u])
            for g, sl in enumerate(heads(0)):
                o = o_all[g * CHUNK:(g + 1) * CHUNK] / dens[u][g]
                o_ref[pl.ds(q0s[u], CHUNK), sl] = o.astype(o_ref.dtype)
        return carry

    lax.fori_loop(0, seq // (CHUNK * ATT_UNROLL), blocks, 0)


_IN_PLACE = pl.BlockSpec(memory_space=pl.ANY)


def _attn_call(proj, slopes, sink, ybuf):
    b, s, _ = proj.shape
    assert s % (CHUNK * ATT_UNROLL) == 0 and s >= 3 * CHUNK, s
    qw = ATT_GROUP * HEAD_DIM
    smem = pl.BlockSpec(memory_space=pltpu.SMEM)
    return pl.pallas_call(
        functools.partial(_attn_kernel, seq=s),
        out_shape=jax.ShapeDtypeStruct(ybuf.shape, ybuf.dtype),
        grid=(b, ATT_KV_HEADS),
        in_specs=[smem, smem,
                  pl.BlockSpec((None, s, qw), lambda i, h: (i, 0, h)),
                  pl.BlockSpec((None, s, HEAD_DIM), lambda i, h: (i, 0, _OFF_AK + h)),
                  pl.BlockSpec((None, s, HEAD_DIM), lambda i, h: (i, 0, _OFF_AV + h)),
                  _IN_PLACE],
        out_specs=pl.BlockSpec((None, None, s, qw), lambda i, h: (0, i, 0, h)),
        input_output_aliases={5: 0},
        compiler_params=_params(2),
        name="windowed_gqa",
    )(slopes, sink, proj, proj, proj, ybuf)


def _ret_kernel(lg_ref, q_ref, k_ref, v_ref, g_ref, ybuf_ref, o_ref, sf_ref, *, seq):
    head = pl.program_id(1)
    lg = lg_ref[head]
    nc = seq // CHUNK
    c = CHUNK
    row = lax.broadcasted_iota(jnp.int32, (c, c), 0)
    col = lax.broadcasted_iota(jnp.int32, (c, c), 1)
    dmat = jnp.exp(lg * jnp.abs(row - col).astype(F32))
    pos = lax.broadcasted_iota(jnp.int32, (c, 1), 0).astype(F32)
    kdec_f = jnp.exp(lg * (c - 1.0 - pos))
    kdec_b = jnp.exp(lg * pos)
    qdec_f = jnp.exp(lg * (pos + 1.0))
    qdec_b = jnp.exp(lg * (c - pos))
    cdec = jnp.exp(lg * c)
    kscale = HEAD_DIM ** -0.5

    def fwd(n, st):
        r0 = pl.multiple_of(n * c, c)
        sf_ref[n] = st
        kc = k_ref[pl.ds(r0, c), :] * kscale
        vc = v_ref[pl.ds(r0, c), :].astype(BF16)
        return st * cdec + _dot_tn((kc * kdec_f).astype(BF16), vc)

    lax.fori_loop(0, nc, fwd, jnp.zeros((c, c), F32), unroll=RET_UNROLL)

    def bwd(i, st):
        n = nc - 1 - i
        r0 = pl.multiple_of(n * c, c)
        qc = q_ref[pl.ds(r0, c), :]
        kc = k_ref[pl.ds(r0, c), :] * kscale
        vc = v_ref[pl.ds(r0, c), :].astype(BF16)
        p = _dot_nt(qc.astype(BF16), kc.astype(BF16)) * dmat
        o = _dot(p.astype(BF16), vc)
        o = o + _dot((qc * qdec_f).astype(BF16), sf_ref[n].astype(BF16))
        o = o + _dot((qc * qdec_b).astype(BF16), st.astype(BF16))
        mu = jnp.mean(o, axis=-1, keepdims=True)
        oc = o - mu
        var = jnp.mean(oc * oc, axis=-1, keepdims=True)
        y = oc * lax.rsqrt(var + LN_EPS) * _silu(g_ref[pl.ds(r0, c), :])
        o_ref[pl.ds(r0, c), :] = y.astype(o_ref.dtype)
        return st * cdec + _dot_tn((kc * kdec_b).astype(BF16), vc)

    lax.fori_loop(0, nc, bwd, jnp.zeros((c, c), F32), unroll=RET_UNROLL)


def _ret_call(proj, log_gamma, ybuf):
    b, s, _ = proj.shape

    def col(off):
        return pl.BlockSpec((None, s, HEAD_DIM), lambda i, h: (i, 0, off + h))

    return pl.pallas_call(
        functools.partial(_ret_kernel, seq=s),
        out_shape=jax.ShapeDtypeStruct(ybuf.shape, ybuf.dtype),
        grid=(b, RET_HEADS),
        in_specs=[pl.BlockSpec(memory_space=pltpu.SMEM), col(_OFF_RQ), col(_OFF_RK), col(_OFF_RV), col(_OFF_RG),
                  _IN_PLACE],
        out_specs=pl.BlockSpec((None, None, s, HEAD_DIM), lambda i, h: (1, i, 0, h)),
        scratch_shapes=[pltpu.VMEM((s // CHUNK, CHUNK, CHUNK), F32)],
        input_output_aliases={5: 0},
        compiler_params=_params(2),
        name="retention",
    )(log_gamma, proj, proj, proj, proj, ybuf)


def _sgu_kernel(u0_ref, u1_ref, v0_ref, v1_ref, w_ref, b_ref, ybuf_ref, o_ref):
    half = u0_ref.shape[-1]
    v = jnp.concatenate([_gelu_tanh(v0_ref[...]), _gelu_tanh(v1_ref[...])], axis=-1)
    mu = jnp.mean(v, axis=-1, keepdims=True)
    vc = v - mu
    var = jnp.mean(vc * vc, axis=-1, keepdims=True)
    vn = (vc * lax.rsqrt(var + LN_EPS)).astype(BF16)
    nchunk = vn.shape[0] // CHUNK
    for g in range(SG_GROUPS):
        lo = g * HEAD_DIM
        vg = jnp.concatenate([vn[r * CHUNK:(r + 1) * CHUNK, lo:lo + HEAD_DIM] for r in range(nchunk)], axis=-1)
        mixed = _dot(w_ref[g], vg) + b_ref[:, g:g + 1]
        u_ref = u0_ref if lo < half else u1_ref
        ul = lo % half
        for r in range(nchunk):
            u = _gelu_tanh(u_ref[r * CHUNK:(r + 1) * CHUNK, ul:ul + HEAD_DIM])
            o_ref[r * CHUNK:(r + 1) * CHUNK, lo:lo + HEAD_DIM] = (
                u * mixed[:, r * HEAD_DIM:(r + 1) * HEAD_DIM]).astype(o_ref.dtype)


def _sgu_call(proj, sg_w, sg_bt, ybuf):
    b, s, _ = proj.shape
    half = SG_GROUPS * HEAD_DIM // 2
    hb = half // HEAD_DIM
    rows = _pick(s, (4 * CHUNK, 2 * CHUNK, CHUNK))

    def blk(off):
        return pl.BlockSpec((None, rows, half), lambda i, n: (i, n, off))

    return pl.pallas_call(
        _sgu_kernel,
        out_shape=jax.ShapeDtypeStruct(ybuf.shape, ybuf.dtype),
        grid=(b, s // rows),
        in_specs=[blk(_OFF_SU // hb), blk(_OFF_SU // hb + 1), blk(_OFF_SV // hb), blk(_OFF_SV // hb + 1),
                  pl.BlockSpec((SG_GROUPS, CHUNK, CHUNK), lambda i, n: (0, 0, 0)),
                  pl.BlockSpec((CHUNK, SG_GROUPS), lambda i, n: (0, 0)),
                  _IN_PLACE],
        out_specs=pl.BlockSpec((None, None, rows, BRANCH_W), lambda i, n: (2, i, n, 0)),
        input_output_aliases={6: 0},
        compiler_params=_params(2),
        name="spatial_gating",
    )(proj, proj, proj, proj, sg_w, sg_bt, ybuf)


def _split3(x):
    x1 = x.astype(BF16)
    r1 = x - x1.astype(F32)
    x2 = r1.astype(BF16)
    x3 = (r1 - x2.astype(F32)).astype(BF16)
    return x1, x2, x3


def _hgrn_tables(forward):
    c = CHUNK
    row = np.arange(c)[:, None]
    col = np.arange(c)[None, :]
    lmat = ((col <= row) if forward else (col >= row)).astype(np.float32)
    qms, lms, lsub = [], [], []
    b = c // 2
    while b >= SUB:
        upper_r = (row % (2 * b)) >= b
        upper_c = (col % (2 * b)) >= b
        q_side = upper_r if forward else ~upper_r
        k_side = ~upper_c if forward else upper_c
        pair = (q_side & k_side & ((row // (2 * b)) == (col // (2 * b)))).astype(np.float32)
        if b >= 8:
            lsub.append(pair[q_side[:, 0]])
        else:
            qms.append(np.broadcast_to(q_side, (c, HEAD_DIM)).astype(np.float32))
            lms.append(pair)
        b //= 2
    off = np.where((row // SUB) == (col // SUB), col - row, 2 * c).astype(np.int32)
    return lmat, np.stack(qms), np.stack(lms), off, np.stack(lsub)


def _hgrn_chunks(qzv, lb, st, tmat, qmask_ref, lmask_ref, lsub_ref, off3, forward):
    c = CHUNK
    tiles = (c // 8, 8, c)
    gates = []
    for q, z, v in qzv:
        f = lb + (1.0 - lb) * _sigmoid(z)
        g1, g2, g3 = _split3(jnp.log2(f))
        a = _dot(tmat, g1) + _dot(tmat, g2) + _dot(tmat, g3)
        gates.append((1.0 - f, a))

    pairs = []
    for (q, z, v), (kk, a) in zip(qzv, gates):
        p_tiles = [None] * (c // 8)

        def add_rows(first_tile, block):
            for i in range(block.shape[0] // 8):
                t = first_tile + i
                x = block[8 * i:8 * i + 8]
                p_tiles[t] = x if p_tiles[t] is None else p_tiles[t] + x

        n_sliced = 0
        for li in range(_HGRN_LEVELS):
            b = c >> (li + 1)
            nb = c // (2 * b)
            a4 = a.reshape(nb, 2 * b, c)
            mid = a4[:, b - 1:b, :] if forward else a4[:, b:b + 1, :]
            if b >= 8:
                qs, ks = (slice(b, 2 * b), slice(0, b)) if forward else (slice(0, b), slice(b, 2 * b))
                q4, k4 = q.reshape(nb, 2 * b, c), kk.reshape(nb, 2 * b, c)
                qt = q4[:, qs, :] * jnp.exp2(a4[:, qs, :] - mid)
                kt = k4[:, ks, :] * jnp.exp2(mid - a4[:, ks, :])
                zero = jnp.zeros_like(kt)
                kt = jnp.concatenate([kt, zero] if forward else [zero, kt], axis=1)
                ps = _dot_nt(qt.reshape(c // 2, c).astype(BF16), kt.reshape(c, c).astype(BF16)) * lsub_ref[n_sliced]
                n_sliced += 1
                for blk in range(nb):
                    add_rows((blk * 2 * b + qs.start) // 8, ps[blk * b:(blk + 1) * b])
            else:
                lm = li - n_sliced
                d = (a4 - mid).reshape(c, c)
                u = ((kk + qmask_ref[lm] * (q - kk)) * jnp.exp2(-jnp.abs(d))).astype(BF16)
                add_rows(0, _dot_nt(u, u) * lmask_ref[lm])

        a3, k3, q3 = a.reshape(tiles), kk.reshape(tiles), q.reshape(tiles)
        p3 = jnp.stack(p_tiles)
        for delta in range(SUB):
            if delta == 0:
                w = q3 * k3
            else:
                sh = delta if forward else 8 - delta
                a_s = pltpu.roll(a3, sh, 1)
                k_s = pltpu.roll(k3, sh, 1)
                w = q3 * k_s * jnp.exp2(jnp.minimum(a3 - a_s, 0.0))
            rs = jnp.sum(w, axis=-1, keepdims=True)
            p3 = jnp.where(off3 == (-delta if forward else delta), rs, p3)
        pairs.append(p3.reshape(c, c).astype(BF16))

    outs = []
    for (q, z, v), (kk, a), p in zip(qzv, gates, pairs):
        a_end = a[c - 1:c] if forward else a[0:1]
        vb = v.astype(BF16)
        o = _dot_nt((q * jnp.exp2(a)).astype(BF16), st.astype(BF16)) + _dot(p, vb)
        st = st * jnp.exp2(a_end) + _dot_tn(vb, (kk * jnp.exp2(a_end - a)).astype(BF16))
        outs.append(o)
    return outs, st


def _hgrn_kernel(q_ref, zf_ref, zb_ref, v_ref, g_ref, lb_ref, tmat_ref, qmask_ref, lmask_ref, lsub_ref, off_ref,
                 ybuf_ref, o_ref, of_ref, *, seq):
    nc = seq // CHUNK
    c = CHUNK
    lb = lb_ref[...]
    off3 = off_ref[...].reshape(c // 8, 8, c)

    def rows(n):
        return pl.ds(pl.multiple_of(n * c, c), c)

    def fwd(it, st):
        ns = [it * HGRN_UNROLL + u for u in range(HGRN_UNROLL)]
        qzv = [(q_ref[rows(n), :], zf_ref[rows(n), :], v_ref[rows(n), :]) for n in ns]
        outs, st = _hgrn_chunks(qzv, lb, st, tmat_ref[0], qmask_ref.at[0], lmask_ref.at[0], lsub_ref.at[0], off3, True)
        for n, o in zip(ns, outs):
            of_ref[rows(n), :] = o
        return st

    lax.fori_loop(0, nc // HGRN_UNROLL, fwd, jnp.zeros((c, c), F32))

    def bwd(it, st):
        ns = [nc - 1 - (it * HGRN_UNROLL + u) for u in range(HGRN_UNROLL)]
        qzv = [(q_ref[rows(n), :], zb_ref[rows(n), :], v_ref[rows(n), :]) for n in ns]
        outs, st = _hgrn_chunks(qzv, lb, st, tmat_ref[1], qmask_ref.at[1], lmask_ref.at[1], lsub_ref.at[1], off3,
                                False)
        for n, o in zip(ns, outs):
            o = o + of_ref[rows(n), :]
            y = o * lax.rsqrt(jnp.mean(o * o, axis=-1, keepdims=True) + LN_EPS) * _silu(g_ref[rows(n), :])
            o_ref[rows(n), :] = y.astype(o_ref.dtype)
        return st

    lax.fori_loop(0, nc // HGRN_UNROLL, bwd, jnp.zeros((c, c), F32))


def _hgrn_call(proj, lb, ybuf):
    b, s, _ = proj.shape
    assert s % (CHUNK * HGRN_UNROLL) == 0, s
    tf, tb = _hgrn_tables(True), _hgrn_tables(False)
    tmat = jnp.asarray(np.stack([tf[0], tb[0]]), BF16)
    qmask = jnp.asarray(np.stack([tf[1], tb[1]]))
    lmask = jnp.asarray(np.stack([tf[2], tb[2]]))
    lsub = jnp.asarray(np.stack([tf[4], tb[4]]))
    off = jnp.asarray(tf[3])

    def col(off_):
        return pl.BlockSpec((None, s, HEAD_DIM), lambda i, h: (i, 0, off_ + h))

    def whole(arr):
        return pl.BlockSpec(arr.shape, lambda i, h: (0,) * arr.ndim)

    return pl.pallas_call(
        functools.partial(_hgrn_kernel, seq=s),
        out_shape=jax.ShapeDtypeStruct(ybuf.shape, ybuf.dtype),
        grid=(b, HGRN_HEADS),
        in_specs=[col(_OFF_DQ), col(_OFF_DFF), col(_OFF_DFB), col(_OFF_DI), col(_OFF_DG),
                  pl.BlockSpec((1, HEAD_DIM), lambda i, h: (0, h)),
                  whole(tmat), whole(qmask), whole(lmask), whole(lsub), whole(off), _IN_PLACE],
        out_specs=pl.BlockSpec((None, None, s, HEAD_DIM), lambda i, h: (3, i, 0, h)),
        scratch_shapes=[pltpu.VMEM((s, HEAD_DIM), F32)],
        input_output_aliases={11: 0},
        compiler_params=_params(2),
        name="hgrn2",
    )(proj, proj, proj, proj, proj, lb, tmat, qmask, lmask, lsub, off, ybuf)


def kernel(x, c, w_in, attn_sink, sg_w, sg_b, hgrn_lb_logits, w_branch, w_gate, w_o, w_mod, b_mod, ln_g, ln_b, w_ffn_in, w_ffn_out):
    bsz, seq, d = x.shape
    depth = w_in.shape[0]
    m = bsz * seq
    alpha = (2.0 * depth) ** 0.25

    pz = jax.nn.softmax(hgrn_lb_logits.astype(F32), axis=0)
    lower_bounds = jnp.cumsum(pz, axis=0) - pz[:1]
    slopes = jnp.exp2(-8.0 * jnp.arange(1, ATT_HEADS + 1, dtype=F32) / ATT_HEADS)
    log_gamma = jnp.log1p(-jnp.exp2(-5.0 - jnp.arange(RET_HEADS, dtype=F32)))

    rows = 8
    c_pad = jnp.zeros((rows, d), F32).at[:bsz].set(c)
    mod = _mod_call(c_pad, w_mod, b_mod)[:, :bsz]

    def mod_part(l, i):
        return mod[l, :, i * d:(i + 1) * d].reshape(bsz, 1, d)

    tm = _pick(m, (1024, 512, 256, 128))
    tn = _pick(d, (512, 256, 128))
    h = _modulate_call(x, mod_part(0, 1), mod_part(0, 0))
    ys = jnp.zeros((N_BRANCH, bsz, seq, BRANCH_W), BF16)
    for l in range(depth):
        h2d = h.reshape(m, d)
        proj = _matmul(h2d, w_in, l, tm=tm, tn=_pick(w_in.shape[2], (512, 256, 128)),
                       out_dtype=F32, name="in_proj").reshape(bsz, seq, -1)
        ys = _attn_call(proj, slopes, attn_sink[l].astype(F32), ys)
        ys = _ret_call(proj, log_gamma, ys)
        ys = _sgu_call(proj, sg_w[l].astype(BF16), sg_b[l].T, ys)
        ys = _hgrn_call(proj, lower_bounds[l].reshape(1, -1), ys)
        merged = _merge_call(h2d, ys.reshape(N_BRANCH, m, BRANCH_W), w_gate, w_branch, l)
        mix = _matmul(merged, w_o, l, tm=tm, tn=tn, out_dtype=BF16, name="out_proj").reshape(bsz, seq, d)
        x, h = _ln_call(x, mix, mod_part(l, 2), ln_g[l, 0:1], ln_b[l, 0:1], alpha, mod_part(l, 4), mod_part(l, 3))
        act = _ffn_in_call(h.reshape(m, d), w_ffn_in, l, _pick(m, (2048, 1024, 512, 256, 128)))
        ffn = _matmul(act, w_ffn_out, l, tm=tm, tn=_pick(d, (256, 128)), out_dtype=BF16, name="ffn_out",
                      lhs_buffers=1).reshape(bsz, seq, d)
        if l + 1 < depth:
            x, h = _ln_call(x, ffn, mod_part(l, 5), ln_g[l, 1:2], ln_b[l, 1:2], alpha,
                            mod_part(l + 1, 1), mod_part(l + 1, 0))
        else:
            x, _ = _ln_call(x, ffn, mod_part(l, 5), ln_g[l, 1:2], ln_b[l, 1:2], alpha)
    return x
```

```python
import functools

import jax
import jax.numpy as jnp
import numpy as np
from jax import lax
from jax.experimental import pallas as pl
from jax.experimental.pallas import tpu as pltpu

F32 = jnp.float32
BF16 = jnp.bfloat16

HEAD_DIM = 128
ATT_HEADS = 8
ATT_KV_HEADS = 2
ATT_GROUP = ATT_HEADS // ATT_KV_HEADS
ATT_WINDOW = 128
RET_HEADS = 8
SG_GROUPS = 8
HGRN_HEADS = 8
CHUNK = 128
TILE_ROWS = 8
SUB = 4
_HGRN_LEVELS = (CHUNK // SUB).bit_length() - 1
HGRN_UNROLL = 8
RET_UNROLL = 8
ATT_UNROLL = 8
BRANCH_W = 8 * HEAD_DIM
N_BRANCH = 4
LN_EPS = 1e-5

_OFF_AQ = 0
_OFF_AK = _OFF_AQ + ATT_HEADS
_OFF_AV = _OFF_AK + ATT_KV_HEADS
_OFF_RQ = _OFF_AV + ATT_KV_HEADS
_OFF_RK = _OFF_RQ + RET_HEADS
_OFF_RV = _OFF_RK + RET_HEADS
_OFF_RG = _OFF_RV + RET_HEADS
_OFF_SU = _OFF_RG + RET_HEADS
_OFF_SV = _OFF_SU + SG_GROUPS
_OFF_DQ = _OFF_SV + SG_GROUPS
_OFF_DFF = _OFF_DQ + HGRN_HEADS
_OFF_DFB = _OFF_DFF + HGRN_HEADS
_OFF_DI = _OFF_DFB + HGRN_HEADS
_OFF_DG = _OFF_DI + HGRN_HEADS

VMEM_LIMIT_BYTES_V7X = 56 * 1024 * 1024


def _params(n_axes):
    return pltpu.CompilerParams(dimension_semantics=("arbitrary",) * n_axes,
                                vmem_limit_bytes=VMEM_LIMIT_BYTES_V7X)


def _dot(a, b):
    return jnp.dot(a, b, preferred_element_type=F32)


def _dot_nt(a, b):
    return lax.dot_general(a, b, (((1,), (1,)), ((), ())), preferred_element_type=F32)


def _dot_tn(a, b):
    return lax.dot_general(a, b, (((0,), (0,)), ((), ())), preferred_element_type=F32)


def _sigmoid(x):
    return 1.0 / (1.0 + jnp.exp(-x))


def _silu(x):
    return x * _sigmoid(x)


def _gelu_tanh(x):
    return 0.5 * x * (1.0 + jnp.tanh(0.7978845608028654 * (x + 0.044715 * (x * x * x))))


def _pick(n, prefs):
    for p in prefs:
        if n % p == 0:
            return p
    return n


def _mod_kernel(c_ref, w_ref, b_ref, o_ref):
    ca = _silu(c_ref[...]).astype(BF16)
    o_ref[...] = _dot(ca, w_ref[...].astype(BF16)) + b_ref[...]


def _mod_call(c_pad, w_mod, b_mod):
    depth, d, n = w_mod.shape
    rows = c_pad.shape[0]
    tn = _pick(n, (512, 256, 128))
    return pl.pallas_call(
        _mod_kernel,
        out_shape=jax.ShapeDtypeStruct((depth, rows, n), F32),
        grid=(depth, n // tn),
        in_specs=[pl.BlockSpec((rows, d), lambda l, j: (0, 0)),
                  pl.BlockSpec((None, d, tn), lambda l, j: (l, 0, j)),
                  pl.BlockSpec((None, 1, tn), lambda l, j: (l, 0, j))],
        out_specs=pl.BlockSpec((None, rows, tn), lambda l, j: (l, 0, j)),
        compiler_params=_params(2),
        name="adaln_mod",
    )(c_pad, w_mod, b_mod.reshape(depth, 1, n))


def _modulate_kernel(x_ref, sc_ref, sh_ref, o_ref):
    o_ref[...] = (x_ref[...] * (1.0 + sc_ref[...]) + sh_ref[...]).astype(o_ref.dtype)


def _modulate_call(x, sc, sh):
    b, s, d = x.shape
    ts = _pick(s, (512, 256, 128))
    return pl.pallas_call(
        _modulate_kernel,
        out_shape=jax.ShapeDtypeStruct((b, s, d), BF16),
        grid=(b, s // ts),
        in_specs=[pl.BlockSpec((None, ts, d), lambda i, j: (i, j, 0)),
                  pl.BlockSpec((None, 1, d), lambda i, j: (i, 0, 0)),
                  pl.BlockSpec((None, 1, d), lambda i, j: (i, 0, 0))],
        out_specs=pl.BlockSpec((None, ts, d), lambda i, j: (i, j, 0)),
        compiler_params=_params(2),
        name="modulate",
    )(x, sc, sh)


def _mm_kernel(x_ref, w_ref, o_ref):
    o_ref[...] = _dot(x_ref[...], w_ref[...].astype(BF16)).astype(o_ref.dtype)


def _matmul(x, w, layer, *, tm, tn, out_dtype, name, lhs_buffers=2):
    m, k = x.shape
    n = w.shape[2]
    return pl.pallas_call(
        _mm_kernel,
        out_shape=jax.ShapeDtypeStruct((m, n), out_dtype),
        grid=(m // tm, n // tn),
        in_specs=[pl.BlockSpec((tm, k), lambda i, j: (i, 0), pipeline_mode=pl.Buffered(lhs_buffers)),
                  pl.BlockSpec((None, k, tn), lambda i, j: (layer, 0, j))],
        out_specs=pl.BlockSpec((tm, tn), lambda i, j: (i, j)),
        compiler_params=_params(2),
        name=name,
    )(x, w)


def _merge_kernel(h_ref, y_ref, wg_ref, wb_ref, o_ref, acc_ref):
    @pl.when(pl.program_id(2) == 0)
    def _():
        acc_ref[...] = jnp.zeros_like(acc_ref)

    gate = _sigmoid(_dot(h_ref[...], wg_ref[...].astype(BF16)))
    acc = acc_ref[...] + gate * _dot(y_ref[...], wb_ref[...].astype(BF16))
    acc_ref[...] = acc
    o_ref[...] = acc.astype(o_ref.dtype)


def _merge_call(h, ys, wg, wb, layer):
    m, d = h.shape
    nb, _, bw = ys.shape
    n = wg.shape[3]
    tm = _pick(m, (1024, 512, 256, 128))
    tn = _pick(n, (512, 256, 128))
    return pl.pallas_call(
        _merge_kernel,
        out_shape=jax.ShapeDtypeStruct((m, n), BF16),
        grid=(m // tm, n // tn, nb),
        in_specs=[pl.BlockSpec((tm, d), lambda i, j, r: (i, 0), pipeline_mode=pl.Buffered(1)),
                  pl.BlockSpec((None, tm, bw), lambda i, j, r: (r, i, 0)),
                  pl.BlockSpec((None, None, d, tn), lambda i, j, r: (layer, r, 0, j)),
                  pl.BlockSpec((None, None, bw, tn), lambda i, j, r: (layer, r, 0, j))],
        out_specs=pl.BlockSpec((tm, tn), lambda i, j, r: (i, j)),
        scratch_shapes=[pltpu.VMEM((tm, tn), F32)],
        compiler_params=_params(3),
        name="gated_merge",
    )(h, ys, wg, wb)


def _residual_ln(x, m, gate, lg, lb, alpha):
    z = alpha * x + (1.0 + gate) * m
    mu = jnp.mean(z, axis=-1, keepdims=True)
    zc = z - mu
    var = jnp.mean(zc * zc, axis=-1, keepdims=True)
    return zc * lax.rsqrt(var + LN_EPS) * lg + lb


def _ln_mod_kernel(x_ref, m_ref, gate_ref, lg_ref, lb_ref, sc_ref, sh_ref, xo_ref, ho_ref, *, alpha):
    xn = _residual_ln(x_ref[...], m_ref[...], gate_ref[...], lg_ref[...], lb_ref[...], alpha)
    xo_ref[...] = xn
    ho_ref[...] = (xn * (1.0 + sc_ref[...]) + sh_ref[...]).astype(ho_ref.dtype)


def _ln_kernel(x_ref, m_ref, gate_ref, lg_ref, lb_ref, xo_ref, *, alpha):
    xo_ref[...] = _residual_ln(x_ref[...], m_ref[...], gate_ref[...], lg_ref[...], lb_ref[...], alpha)


def _ln_call(x, m, gate, ln_g, ln_b, alpha, sc=None, sh=None):
    b, s, d = x.shape
    ts = _pick(s, (256, 128))
    big = pl.BlockSpec((None, ts, d), lambda i, j: (i, j, 0))
    per_b = pl.BlockSpec((None, 1, d), lambda i, j: (i, 0, 0))
    shared = pl.BlockSpec((1, d), lambda i, j: (0, 0))
    x_shape = jax.ShapeDtypeStruct((b, s, d), F32)
    if sc is None:
        return pl.pallas_call(
            functools.partial(_ln_kernel, alpha=alpha),
            out_shape=x_shape,
            grid=(b, s // ts),
            in_specs=[big, big, per_b, shared, shared],
            out_specs=big,
            compiler_params=_params(2),
            name="residual_ln_last",
        )(x, m, gate, ln_g, ln_b), None
    return pl.pallas_call(
        functools.partial(_ln_mod_kernel, alpha=alpha),
        out_shape=(x_shape, jax.ShapeDtypeStruct((b, s, d), BF16)),
        grid=(b, s // ts),
        in_specs=[big, big, per_b, shared, shared, per_b, per_b],
        out_specs=(big, big),
        compiler_params=_params(2),
        name="residual_ln",
    )(x, m, gate, ln_g, ln_b, sc, sh)


def _ffn_in_kernel(h_ref, wa_ref, wg_ref, o_ref):
    h = h_ref[...]
    a = _dot(h, wa_ref[...].astype(BF16))
    g = _dot(h, wg_ref[...].astype(BF16))
    o_ref[...] = (_silu(a) * g).astype(o_ref.dtype)


def _ffn_in_call(h, w, layer, tm):
    m, d = h.shape
    f = w.shape[2] // 2
    tn = _pick(f, (256, 128))
    nj = f // tn
    return pl.pallas_call(
        _ffn_in_kernel,
        out_shape=jax.ShapeDtypeStruct((m, f), BF16),
        grid=(m // tm, nj),
        in_specs=[pl.BlockSpec((tm, d), lambda i, j: (i, 0), pipeline_mode=pl.Buffered(1)),
                  pl.BlockSpec((None, d, tn), lambda i, j: (layer, 0, j)),
                  pl.BlockSpec((None, d, tn), lambda i, j: (layer, 0, j + nj))],
        out_specs=pl.BlockSpec((tm, tn), lambda i, j: (i, j)),
        compiler_params=_params(2),
        name="swiglu_in",
    )(h, w, w)


def _attn_kernel(slope_ref, sink_ref, q_ref, k_ref, v_ref, ybuf_ref, o_ref, *, seq):
    hkv = pl.program_id(1)
    span = 3 * CHUNK
    scale = HEAD_DIM ** -0.5

    def heads(lo):
        return [slice(lo + g * HEAD_DIM, lo + (g + 1) * HEAD_DIM) for g in range(ATT_GROUP)]

    def blocks(it, carry):
        q0s, vbs, scs = [], [], []
        for u in range(ATT_UNROLL):
            q0 = pl.multiple_of((it * ATT_UNROLL + u) * CHUNK, CHUNK)
            ks = pl.multiple_of(jnp.clip(q0 - CHUNK, 0, seq - span), CHUNK)
            kb = k_ref[pl.ds(ks, span), :].astype(BF16)
            vbs.append(v_ref[pl.ds(ks, span), :].astype(BF16))
            qs = jnp.concatenate([q_ref[pl.ds(q0, CHUNK), sl] for sl in heads(0)], axis=0).astype(BF16)
            scs.append((_dot_nt(qs, kb) * scale, ks))
            q0s.append(q0)
        ps, dens = [], []
        for u in range(ATT_UNROLL):
            sc_all, ks = scs[u]
            qpos = q0s[u] + lax.broadcasted_iota(jnp.int32, (CHUNK, span), 0)
            kpos = ks + lax.broadcasted_iota(jnp.int32, (CHUNK, span), 1)
            dist = jnp.abs(qpos - kpos)
            valid = dist <= ATT_WINDOW
            distf = dist.astype(F32)
            pg, dg = [], []
            for g in range(ATT_GROUP):
                head = hkv * ATT_GROUP + g
                sc = sc_all[g * CHUNK:(g + 1) * CHUNK] - slope_ref[head] * distf
                sc = jnp.where(valid, sc, -jnp.inf)
                sink = sink_ref[head]
                mx = jnp.maximum(jnp.max(sc, axis=-1, keepdims=True), sink)
                p = jnp.exp(sc - mx)
                dg.append(jnp.sum(p, axis=-1, keepdims=True) + jnp.exp(sink - mx))
                pg.append(p.astype(BF16))
            ps.append(jnp.concatenate(pg, axis=0))
            dens.append(dg)
        for u in range(ATT_UNROLL):
            o_all = _dot(ps[u], vbs[u])
            for g, sl in enumerate(heads(0)):
                o = o_all[g * CHUNK:(g + 1) * CHUNK] / dens[u][g]
                o_ref[pl.ds(q0s[u], CHUNK), sl] = o.astype(o_ref.dtype)
        return carry

    lax.fori_loop(0, seq // (CHUNK * ATT_UNROLL), blocks, 0)


_IN_PLACE = pl.BlockSpec(memory_space=pl.ANY)


def _attn_call(proj, slopes, sink, ybuf):
    b, s, _ = proj.shape
    assert s % (CHUNK * ATT_UNROLL) == 0 and s >= 3 * CHUNK, s
    qw = ATT_GROUP * HEAD_DIM
    smem = pl.BlockSpec(memory_space=pltpu.SMEM)
    return pl.pallas_call(
        functools.partial(_attn_kernel, seq=s),
        out_shape=jax.ShapeDtypeStruct(ybuf.shape, ybuf.dtype),
        grid=(b, ATT_KV_HEADS),
        in_specs=[smem, smem,
                  pl.BlockSpec((None, s, qw), lambda i, h: (i, 0, h)),
                  pl.BlockSpec((None, s, HEAD_DIM), lambda i, h: (i, 0, _OFF_AK + h)),
                  pl.BlockSpec((None, s, HEAD_DIM), lambda i, h: (i, 0, _OFF_AV + h)),
                  _IN_PLACE],
        out_specs=pl.BlockSpec((None, None, s, qw), lambda i, h: (0, i, 0, h)),
        input_output_aliases={5: 0},
        compiler_params=_params(2),
        name="windowed_gqa",
    )(slopes, sink, proj, proj, proj, ybuf)


def _ret_kernel(lg_ref, q_ref, k_ref, v_ref, g_ref, ybuf_ref, o_ref, sf_ref, *, seq):
    head = pl.program_id(1)
    lg = lg_ref[head]
    nc = seq // CHUNK
    c = CHUNK
    row = lax.broadcasted_iota(jnp.int32, (c, c), 0)
    col = lax.broadcasted_iota(jnp.int32, (c, c), 1)
    dmat = jnp.exp(lg * jnp.abs(row - col).astype(F32))
    pos = lax.broadcasted_iota(jnp.int32, (c, 1), 0).astype(F32)
    kdec_f = jnp.exp(lg * (c - 1.0 - pos))
    kdec_b = jnp.exp(lg * pos)
    qdec_f = jnp.exp(lg * (pos + 1.0))
    qdec_b = jnp.exp(lg * (c - pos))
    cdec = jnp.exp(lg * c)
    kscale = HEAD_DIM ** -0.5

    def fwd(n, st):
        r0 = pl.multiple_of(n * c, c)
        sf_ref[n] = st
        kc = k_ref[pl.ds(r0, c), :] * kscale
        vc = v_ref[pl.ds(r0, c), :].astype(BF16)
        return st * cdec + _dot_tn((kc * kdec_f).astype(BF16), vc)

    lax.fori_loop(0, nc, fwd, jnp.zeros((c, c), F32), unroll=RET_UNROLL)

    def bwd(i, st):
        n = nc - 1 - i
        r0 = pl.multiple_of(n * c, c)
        qc = q_ref[pl.ds(r0, c), :]
        kc = k_ref[pl.ds(r0, c), :] * kscale
        vc = v_ref[pl.ds(r0, c), :].astype(BF16)
        p = _dot_nt(qc.astype(BF16), kc.astype(BF16)) * dmat
        o = _dot(p.astype(BF16), vc)
        o = o + _dot((qc * qdec_f).astype(BF16), sf_ref[n].astype(BF16))
        o = o + _dot((qc * qdec_b).astype(BF16), st.astype(BF16))
        mu = jnp.mean(o, axis=-1, keepdims=True)
        oc = o - mu
        var = jnp.mean(oc * oc, axis=-1, keepdims=True)
        y = oc * lax.rsqrt(var + LN_EPS) * _silu(g_ref[pl.ds(r0, c), :])
        o_ref[pl.ds(r0, c), :] = y.astype(o_ref.dtype)
        return st * cdec + _dot_tn((kc * kdec_b).astype(BF16), vc)

    lax.fori_loop(0, nc, bwd, jnp.zeros((c, c), F32), unroll=RET_UNROLL)


def _ret_call(proj, log_gamma, ybuf):
    b, s, _ = proj.shape

    def col(off):
        return pl.BlockSpec((None, s, HEAD_DIM), lambda i, h: (i, 0, off + h))

    return pl.pallas_call(
        functools.partial(_ret_kernel, seq=s),
        out_shape=jax.ShapeDtypeStruct(ybuf.shape, ybuf.dtype),
        grid=(b, RET_HEADS),
        in_specs=[pl.BlockSpec(memory_space=pltpu.SMEM), col(_OFF_RQ), col(_OFF_RK), col(_OFF_RV), col(_OFF_RG),
                  _IN_PLACE],
        out_specs=pl.BlockSpec((None, None, s, HEAD_DIM), lambda i, h: (1, i, 0, h)),
        scratch_shapes=[pltpu.VMEM((s // CHUNK, CHUNK, CHUNK), F32)],
        input_output_aliases={5: 0},
        compiler_params=_params(2),
        name="retention",
    )(log_gamma, proj, proj, proj, proj, ybuf)


def _sgu_kernel(u0_ref, u1_ref, v0_ref, v1_ref, w_ref, b_ref, ybuf_ref, o_ref):
    half = u0_ref.shape[-1]
    v = jnp.concatenate([_gelu_tanh(v0_ref[...]), _gelu_tanh(v1_ref[...])], axis=-1)
    mu = jnp.mean(v, axis=-1, keepdims=True)
    vc = v - mu
    var = jnp.mean(vc * vc, axis=-1, keepdims=True)
    vn = (vc * lax.rsqrt(var + LN_EPS)).astype(BF16)
    nchunk = vn.shape[0] // CHUNK
    for g in range(SG_GROUPS):
        lo = g * HEAD_DIM
        vg = jnp.concatenate([vn[r * CHUNK:(r + 1) * CHUNK, lo:lo + HEAD_DIM] for r in range(nchunk)], axis=-1)
        mixed = _dot(w_ref[g], vg) + b_ref[:, g:g + 1]
        u_ref = u0_ref if lo < half else u1_ref
        ul = lo % half
        for r in range(nchunk):
            u = _gelu_tanh(u_ref[r * CHUNK:(r + 1) * CHUNK, ul:ul + HEAD_DIM])
            o_ref[r * CHUNK:(r + 1) * CHUNK, lo:lo + HEAD_DIM] = (
                u * mixed[:, r * HEAD_DIM:(r + 1) * HEAD_DIM]).astype(o_ref.dtype)


def _sgu_call(proj, sg_w, sg_bt, ybuf):
    b, s, _ = proj.shape
    half = SG_GROUPS * HEAD_DIM // 2
    hb = half // HEAD_DIM
    rows = _pick(s, (4 * CHUNK, 2 * CHUNK, CHUNK))

    def blk(off):
        return pl.BlockSpec((None, rows, half), lambda i, n: (i, n, off))

    return pl.pallas_call(
        _sgu_kernel,
        out_shape=jax.ShapeDtypeStruct(ybuf.shape, ybuf.dtype),
        grid=(b, s // rows),
        in_specs=[blk(_OFF_SU // hb), blk(_OFF_SU // hb + 1), blk(_OFF_SV // hb), blk(_OFF_SV // hb + 1),
                  pl.BlockSpec((SG_GROUPS, CHUNK, CHUNK), lambda i, n: (0, 0, 0)),
                  pl.BlockSpec((CHUNK, SG_GROUPS), lambda i, n: (0, 0)),
                  _IN_PLACE],
        out_specs=pl.BlockSpec((None, None, rows, BRANCH_W), lambda i, n: (2, i, n, 0)),
        input_output_aliases={6: 0},
        compiler_params=_params(2),
        name="spatial_gating",
    )(proj, proj, proj, proj, sg_w, sg_bt, ybuf)


def _split3(x):
    x1 = x.astype(BF16)
    r1 = x - x1.astype(F32)
    x2 = r1.astype(BF16)
    x3 = (r1 - x2.astype(F32)).astype(BF16)
    return x1, x2, x3


def _hgrn_tables(forward):
    c = CHUNK
    row = np.arange(c)[:, None]
    col = np.arange(c)[None, :]
    lmat = ((col <= row) if forward else (col >= row)).astype(np.float32)
    qms, lms, lsub = [], [], []
    b = c // 2
    while b >= SUB:
        upper_r = (row % (2 * b)) >= b
        upper_c = (col % (2 * b)) >= b
        q_side = upper_r if forward else ~upper_r
        k_side = ~upper_c if forward else upper_c
        pair = (q_side & k_side & ((row // (2 * b)) == (col // (2 * b)))).astype(np.float32)
        if b >= TILE_ROWS:
            lsub.append(pair[q_side[:, 0]])
        else:
            qms.append(np.broadcast_to(q_side, (c, HEAD_DIM)).astype(np.float32))
            lms.append(pair)
        b //= 2
    off = np.where((row // SUB) == (col // SUB), col - row, 2 * c).astype(np.int32)
    return lmat, np.stack(qms), np.stack(lms), off, np.stack(lsub)


def _hgrn_chunks(qzv, lb, st, tmat, qmask_ref, lmask_ref, lsub_ref, off3, forward):
    c = CHUNK
    tr = TILE_ROWS
    tiles = (c // tr, tr, c)
    gates = []
    for q, z, v in qzv:
        f = lb + (1.0 - lb) * _sigmoid(z)
        g1, g2, g3 = _split3(jnp.log2(f))
        a = _dot(tmat, g1) + _dot(tmat, g2) + _dot(tmat, g3)
        gates.append((1.0 - f, a))

    pairs = []
    for (q, z, v), (kk, a) in zip(qzv, gates):
        p_tiles = [None] * (c // tr)

        def add_rows(first_tile, block):
            for i in range(block.shape[0] // tr):
                t = first_tile + i
                x = block[tr * i:tr * (i + 1)]
                p_tiles[t] = x if p_tiles[t] is None else p_tiles[t] + x

        n_sliced = 0
        for li in range(_HGRN_LEVELS):
            b = c >> (li + 1)
            nb = c // (2 * b)
            a4 = a.reshape(nb, 2 * b, c)
            mid = a4[:, b - 1:b, :] if forward else a4[:, b:b + 1, :]
            if b >= tr:
                qs, ks = (slice(b, 2 * b), slice(0, b)) if forward else (slice(0, b), slice(b, 2 * b))
                q4, k4 = q.reshape(nb, 2 * b, c), kk.reshape(nb, 2 * b, c)
                qt = q4[:, qs, :] * jnp.exp2(a4[:, qs, :] - mid)
                kt = k4[:, ks, :] * jnp.exp2(mid - a4[:, ks, :])
                zero = jnp.zeros_like(kt)
                kt = jnp.concatenate([kt, zero] if forward else [zero, kt], axis=1)
                ps = _dot_nt(qt.reshape(c // 2, c).astype(BF16), kt.reshape(c, c).astype(BF16)) * lsub_ref[n_sliced]
                n_sliced += 1
                for blk in range(nb):
                    add_rows((blk * 2 * b + qs.start) // tr, ps[blk * b:(blk + 1) * b])
            else:
                lm = li - n_sliced
                d = (a4 - mid).reshape(c, c)
                u = ((kk + qmask_ref[lm] * (q - kk)) * jnp.exp2(-jnp.abs(d))).astype(BF16)
                add_rows(0, _dot_nt(u, u) * lmask_ref[lm])

        a3, k3, q3 = a.reshape(tiles), kk.reshape(tiles), q.reshape(tiles)
        p3 = jnp.stack(p_tiles)
        for delta in range(SUB):
            if delta == 0:
                w = q3 * k3
            else:
                sh = delta if forward else tr - delta
                a_s = pltpu.roll(a3, sh, 1)
                k_s = pltpu.roll(k3, sh, 1)
                w = q3 * k_s * jnp.exp2(jnp.minimum(a3 - a_s, 0.0))
            rs = jnp.sum(w, axis=-1, keepdims=True)
            p3 = jnp.where(off3 == (-delta if forward else delta), rs, p3)
        pairs.append(p3.reshape(c, c).astype(BF16))

    outs = []
    for (q, z, v), (kk, a), p in zip(qzv, gates, pairs):
        a_end = a[c - 1:c] if forward else a[0:1]
        vb = v.astype(BF16)
        o = _dot_nt((q * jnp.exp2(a)).astype(BF16), st.astype(BF16)) + _dot(p, vb)
        st = st * jnp.exp2(a_end) + _dot_tn(vb, (kk * jnp.exp2(a_end - a)).astype(BF16))
        outs.append(o)
    return outs, st


def _hgrn_kernel(q_ref, zf_ref, zb_ref, v_ref, g_ref, lb_ref, tmat_ref, qmask_ref, lmask_ref, lsub_ref, off_ref,
                 ybuf_ref, o_ref, of_ref, *, seq):
    nc = seq // CHUNK
    c = CHUNK
    lb = lb_ref[...]
    off3 = off_ref[...].reshape(c // TILE_ROWS, TILE_ROWS, c)

    def rows(n):
        return pl.ds(pl.multiple_of(n * c, c), c)

    def fwd(it, st):
        ns = [it * HGRN_UNROLL + u for u in range(HGRN_UNROLL)]
        qzv = [(q_ref[rows(n), :], zf_ref[rows(n), :], v_ref[rows(n), :]) for n in ns]
        outs, st = _hgrn_chunks(qzv, lb, st, tmat_ref[0], qmask_ref.at[0], lmask_ref.at[0], lsub_ref.at[0], off3, True)
        for n, o in zip(ns, outs):
            of_ref[rows(n), :] = o
        return st

    lax.fori_loop(0, nc // HGRN_UNROLL, fwd, jnp.zeros((c, c), F32))

    def bwd(it, st):
        ns = [nc - 1 - (it * HGRN_UNROLL + u) for u in range(HGRN_UNROLL)]
        qzv = [(q_ref[rows(n), :], zb_ref[rows(n), :], v_ref[rows(n), :]) for n in ns]
        outs, st = _hgrn_chunks(qzv, lb, st, tmat_ref[1], qmask_ref.at[1], lmask_ref.at[1], lsub_ref.at[1], off3,
                                False)
        for n, o in zip(ns, outs):
            o = o + of_ref[rows(n), :]
            y = o * lax.rsqrt(jnp.mean(o * o, axis=-1, keepdims=True) + LN_EPS) * _silu(g_ref[rows(n), :])
            o_ref[rows(n), :] = y.astype(o_ref.dtype)
        return st

    lax.fori_loop(0, nc // HGRN_UNROLL, bwd, jnp.zeros((c, c), F32))


def _hgrn_call(proj, lb, ybuf):
    b, s, _ = proj.shape
    assert s % (CHUNK * HGRN_UNROLL) == 0, s
    tf, tb = _hgrn_tables(True), _hgrn_tables(False)
    tmat = jnp.asarray(np.stack([tf[0], tb[0]]), BF16)
    qmask = jnp.asarray(np.stack([tf[1], tb[1]]))
    lmask = jnp.asarray(np.stack([tf[2], tb[2]]))
    lsub = jnp.asarray(np.stack([tf[4], tb[4]]))
    off = jnp.asarray(tf[3])

    def col(off_):
        return pl.BlockSpec((None, s, HEAD_DIM), lambda i, h: (i, 0, off_ + h))

    def whole(arr):
        return pl.BlockSpec(arr.shape, lambda i, h: (0,) * arr.ndim)

    return pl.pallas_call(
        functools.partial(_hgrn_kernel, seq=s),
        out_shape=jax.ShapeDtypeStruct(ybuf.shape, ybuf.dtype),
        grid=(b, HGRN_HEADS),
        in_specs=[col(_OFF_DQ), col(_OFF_DFF), col(_OFF_DFB), col(_OFF_DI), col(_OFF_DG),
                  pl.BlockSpec((1, HEAD_DIM), lambda i, h: (0, h)),
                  whole(tmat), whole(qmask), whole(lmask), whole(lsub), whole(off), _IN_PLACE],
        out_specs=pl.BlockSpec((None, None, s, HEAD_DIM), lambda i, h: (3, i, 0, h)),
        scratch_shapes=[pltpu.VMEM((s, HEAD_DIM), F32)],
        input_output_aliases={11: 0},
        compiler_params=_params(2),
        name="hgrn2",
    )(proj, proj, proj, proj, proj, lb, tmat, qmask, lmask, lsub, off, ybuf)


def kernel(x, c, w_in, attn_sink, sg_w, sg_b, hgrn_lb_logits, w_branch, w_gate, w_o, w_mod, b_mod, ln_g, ln_b, w_ffn_in, w_ffn_out):
    bsz, seq, d = x.shape
    depth = w_in.shape[0]
    m = bsz * seq
    alpha = (2.0 * depth) ** 0.25

    pz = jax.nn.softmax(hgrn_lb_logits.astype(F32), axis=0)
    lower_bounds = jnp.cumsum(pz, axis=0) - pz[:1]
    slopes = jnp.exp2(-8.0 * jnp.arange(1, ATT_HEADS + 1, dtype=F32) / ATT_HEADS)
    log_gamma = jnp.log1p(-jnp.exp2(-5.0 - jnp.arange(RET_HEADS, dtype=F32)))

    assert bsz <= TILE_ROWS, bsz
    c_pad = jnp.zeros((TILE_ROWS, d), F32).at[:bsz].set(c)
    mod = _mod_call(c_pad, w_mod, b_mod)[:, :bsz]

    def mod_part(l, i):
        return mod[l, :, i * d:(i + 1) * d].reshape(bsz, 1, d)

    tm = _pick(m, (1024, 512, 256, 128))
    tn = _pick(d, (512, 256, 128))
    h = _modulate_call(x, mod_part(0, 1), mod_part(0, 0))
    ys = jnp.zeros((N_BRANCH, bsz, seq, BRANCH_W), BF16)
    for l in range(depth):
        h2d = h.reshape(m, d)
        proj = _matmul(h2d, w_in, l, tm=tm, tn=_pick(w_in.shape[2], (512, 256, 128)),
                       out_dtype=F32, name="in_proj").reshape(bsz, seq, -1)
        ys = _attn_call(proj, slopes, attn_sink[l].astype(F32), ys)
        ys = _ret_call(proj, log_gamma, ys)
        ys = _sgu_call(proj, sg_w[l].astype(BF16), sg_b[l].T, ys)
        ys = _hgrn_call(proj, lower_bounds[l].reshape(1, -1), ys)
        merged = _merge_call(h2d, ys.reshape(N_BRANCH, m, BRANCH_W), w_gate, w_branch, l)
        mix = _matmul(merged, w_o, l, tm=tm, tn=tn, out_dtype=BF16, name="out_proj").reshape(bsz, seq, d)
        x, h = _ln_call(x, mix, mod_part(l, 2), ln_g[l, 0:1], ln_b[l, 0:1], alpha, mod_part(l, 4), mod_part(l, 3))
        act = _ffn_in_call(h.reshape(m, d), w_ffn_in, l, _pick(m, (2048, 1024, 512, 256, 128)))
        ffn = _matmul(act, w_ffn_out, l, tm=tm, tn=_pick(d, (256, 128)), out_dtype=BF16, name="ffn_out",
                      lhs_buffers=1).reshape(bsz, seq, d)
        if l + 1 < depth:
            x, h = _ln_call(x, ffn, mod_part(l, 5), ln_g[l, 1:2], ln_b[l, 1:2], alpha,
                            mod_part(l + 1, 1), mod_part(l + 1, 0))
        else:
            x, _ = _ln_call(x, ffn, mod_part(l, 5), ln_g[l, 1:2], ln_b[l, 1:2], alpha)
    return x
```

```python
import functools

import jax
import jax.numpy as jnp
import numpy as np
from jax import lax
from jax.experimental import pallas as pl
from jax.experimental.pallas import tpu as pltpu

F32 = jnp.float32
BF16 = jnp.bfloat16

HEAD_DIM = 128
ATT_HEADS = 8
ATT_KV_HEADS = 2
ATT_GROUP = ATT_HEADS // ATT_KV_HEADS
ATT_WINDOW = 128
RET_HEADS = 8
SG_GROUPS = 8
HGRN_HEADS = 8
CHUNK = 128
TILE_ROWS = 8
SUB = 4
_HGRN_LEVELS = (CHUNK // SUB).bit_length() - 1
HGRN_UNROLL = 8
RET_UNROLL = 8
ATT_UNROLL = 8
BRANCH_W = 8 * HEAD_DIM
N_BRANCH = 4
LN_EPS = 1e-5

_OFF_AQ = 0
_OFF_AK = _OFF_AQ + ATT_HEADS
_OFF_AV = _OFF_AK + ATT_KV_HEADS
_OFF_RQ = _OFF_AV + ATT_KV_HEADS
_OFF_RK = _OFF_RQ + RET_HEADS
_OFF_RV = _OFF_RK + RET_HEADS
_OFF_RG = _OFF_RV + RET_HEADS
_OFF_SU = _OFF_RG + RET_HEADS
_OFF_SV = _OFF_SU + SG_GROUPS
_OFF_DQ = _OFF_SV + SG_GROUPS
_OFF_DFF = _OFF_DQ + HGRN_HEADS
_OFF_DFB = _OFF_DFF + HGRN_HEADS
_OFF_DI = _OFF_DFB + HGRN_HEADS
_OFF_DG = _OFF_DI + HGRN_HEADS

VMEM_LIMIT_BYTES_V7X = 56 * 1024 * 1024


def _params(n_axes):
    return pltpu.CompilerParams(dimension_semantics=("arbitrary",) * n_axes,
                                vmem_limit_bytes=VMEM_LIMIT_BYTES_V7X)


def _dot(a, b):
    return jnp.dot(a, b, preferred_element_type=F32)


def _dot_nt(a, b):
    return lax.dot_general(a, b, (((1,), (1,)), ((), ())), preferred_element_type=F32)


def _dot_tn(a, b):
    return lax.dot_general(a, b, (((0,), (0,)), ((), ())), preferred_element_type=F32)


def _sigmoid(x):
    return 1.0 / (1.0 + jnp.exp(-x))


def _silu(x):
    return x * _sigmoid(x)


def _gelu_tanh(x):
    return 0.5 * x * (1.0 + jnp.tanh(0.7978845608028654 * (x + 0.044715 * (x * x * x))))


def _pick(n, prefs):
    for p in prefs:
        if n % p == 0:
            return p
    return n


def _mod_kernel(c_ref, w_ref, b_ref, o_ref):
    ca = _silu(c_ref[...]).astype(BF16)
    o_ref[...] = _dot(ca, w_ref[...].astype(BF16)) + b_ref[...]


def _mod_call(c_pad, w_mod, b_mod):
    depth, d, n = w_mod.shape
    rows = c_pad.shape[0]
    tn = _pick(n, (512, 256, 128))
    return pl.pallas_call(
        _mod_kernel,
        out_shape=jax.ShapeDtypeStruct((depth, rows, n), F32),
        grid=(depth, n // tn),
        in_specs=[pl.BlockSpec((rows, d), lambda l, j: (0, 0)),
                  pl.BlockSpec((None, d, tn), lambda l, j: (l, 0, j)),
                  pl.BlockSpec((None, 1, tn), lambda l, j: (l, 0, j))],
        out_specs=pl.BlockSpec((None, rows, tn), lambda l, j: (l, 0, j)),
        compiler_params=_params(2),
        name="adaln_mod",
    )(c_pad, w_mod, b_mod.reshape(depth, 1, n))


def _modulate_kernel(x_ref, sc_ref, sh_ref, o_ref):
    o_ref[...] = (x_ref[...] * (1.0 + sc_ref[...]) + sh_ref[...]).astype(o_ref.dtype)


def _modulate_call(x, sc, sh):
    b, s, d = x.shape
    ts = _pick(s, (512, 256, 128))
    return pl.pallas_call(
        _modulate_kernel,
        out_shape=jax.ShapeDtypeStruct((b, s, d), BF16),
        grid=(b, s // ts),
        in_specs=[pl.BlockSpec((None, ts, d), lambda i, j: (i, j, 0)),
                  pl.BlockSpec((None, 1, d), lambda i, j: (i, 0, 0)),
                  pl.BlockSpec((None, 1, d), lambda i, j: (i, 0, 0))],
        out_specs=pl.BlockSpec((None, ts, d), lambda i, j: (i, j, 0)),
        compiler_params=_params(2),
        name="modulate",
    )(x, sc, sh)


def _mm_kernel(x_ref, w_ref, o_ref):
    o_ref[...] = _dot(x_ref[...], w_ref[...].astype(BF16)).astype(o_ref.dtype)


def _matmul(x, w, layer, *, tm, tn, out_dtype, name, lhs_buffers=2):
    m, k = x.shape
    n = w.shape[2]
    return pl.pallas_call(
        _mm_kernel,
        out_shape=jax.ShapeDtypeStruct((m, n), out_dtype),
        grid=(m // tm, n // tn),
        in_specs=[pl.BlockSpec((tm, k), lambda i, j: (i, 0), pipeline_mode=pl.Buffered(lhs_buffers)),
                  pl.BlockSpec((None, k, tn), lambda i, j: (layer, 0, j))],
        out_specs=pl.BlockSpec((tm, tn), lambda i, j: (i, j)),
        compiler_params=_params(2),
        name=name,
    )(x, w)


def _merge_kernel(h_ref, y_ref, wg_ref, wb_ref, o_ref, acc_ref):
    @pl.when(pl.program_id(2) == 0)
    def _():
        acc_ref[...] = jnp.zeros_like(acc_ref)

    gate = _sigmoid(_dot(h_ref[...], wg_ref[...].astype(BF16)))
    acc = acc_ref[...] + gate * _dot(y_ref[...], wb_ref[...].astype(BF16))
    acc_ref[...] = acc
    o_ref[...] = acc.astype(o_ref.dtype)


def _merge_call(h, ys, wg, wb, layer):
    m, d = h.shape
    nb, _, bw = ys.shape
    n = wg.shape[3]
    tm = _pick(m, (1024, 512, 256, 128))
    tn = _pick(n, (512, 256, 128))
    return pl.pallas_call(
        _merge_kernel,
        out_shape=jax.ShapeDtypeStruct((m, n), BF16),
        grid=(m // tm, n // tn, nb),
        in_specs=[pl.BlockSpec((tm, d), lambda i, j, r: (i, 0)),
                  pl.BlockSpec((None, tm, bw), lambda i, j, r: (r, i, 0)),
                  pl.BlockSpec((None, None, d, tn), lambda i, j, r: (layer, r, 0, j)),
                  pl.BlockSpec((None, None, bw, tn), lambda i, j, r: (layer, r, 0, j))],
        out_specs=pl.BlockSpec((tm, tn), lambda i, j, r: (i, j)),
        scratch_shapes=[pltpu.VMEM((tm, tn), F32)],
        compiler_params=_params(3),
        name="gated_merge",
    )(h, ys, wg, wb)


def _residual_ln(x, m, gate, lg, lb, alpha):
    z = alpha * x + (1.0 + gate) * m
    mu = jnp.mean(z, axis=-1, keepdims=True)
    zc = z - mu
    var = jnp.mean(zc * zc, axis=-1, keepdims=True)
    return zc * lax.rsqrt(var + LN_EPS) * lg + lb


def _ln_mod_kernel(x_ref, m_ref, gate_ref, lg_ref, lb_ref, sc_ref, sh_ref, xo_ref, ho_ref, *, alpha):
    xn = _residual_ln(x_ref[...], m_ref[...], gate_ref[...], lg_ref[...], lb_ref[...], alpha)
    xo_ref[...] = xn
    ho_ref[...] = (xn * (1.0 + sc_ref[...]) + sh_ref[...]).astype(ho_ref.dtype)


def _ln_kernel(x_ref, m_ref, gate_ref, lg_ref, lb_ref, xo_ref, *, alpha):
    xo_ref[...] = _residual_ln(x_ref[...], m_ref[...], gate_ref[...], lg_ref[...], lb_ref[...], alpha)


def _ln_call(x, m, gate, ln_g, ln_b, alpha, sc=None, sh=None):
    b, s, d = x.shape
    ts = _pick(s, (256, 128))
    big = pl.BlockSpec((None, ts, d), lambda i, j: (i, j, 0))
    per_b = pl.BlockSpec((None, 1, d), lambda i, j: (i, 0, 0))
    shared = pl.BlockSpec((1, d), lambda i, j: (0, 0))
    x_shape = jax.ShapeDtypeStruct((b, s, d), F32)
    if sc is None:
        return pl.pallas_call(
            functools.partial(_ln_kernel, alpha=alpha),
            out_shape=x_shape,
            grid=(b, s // ts),
            in_specs=[big, big, per_b, shared, shared],
            out_specs=big,
            compiler_params=_params(2),
            name="residual_ln_last",
        )(x, m, gate, ln_g, ln_b), None
    return pl.pallas_call(
        functools.partial(_ln_mod_kernel, alpha=alpha),
        out_shape=(x_shape, jax.ShapeDtypeStruct((b, s, d), BF16)),
        grid=(b, s // ts),
        in_specs=[big, big, per_b, shared, shared, per_b, per_b],
        out_specs=(big, big),
        compiler_params=_params(2),
        name="residual_ln",
    )(x, m, gate, ln_g, ln_b, sc, sh)


def _ffn_in_kernel(h_ref, wa_ref, wg_ref, o_ref):
    h = h_ref[...]
    a = _dot(h, wa_ref[...].astype(BF16))
    g = _dot(h, wg_ref[...].astype(BF16))
    o_ref[...] = (_silu(a) * g).astype(o_ref.dtype)


def _ffn_in_call(h, w, layer, tm):
    m, d = h.shape
    f = w.shape[2] // 2
    tn = _pick(f, (256, 128))
    nj = f // tn
    return pl.pallas_call(
        _ffn_in_kernel,
        out_shape=jax.ShapeDtypeStruct((m, f), BF16),
        grid=(m // tm, nj),
        in_specs=[pl.BlockSpec((tm, d), lambda i, j: (i, 0)),
                  pl.BlockSpec((None, d, tn), lambda i, j: (layer, 0, j)),
                  pl.BlockSpec((None, d, tn), lambda i, j: (layer, 0, j + nj))],
        out_specs=pl.BlockSpec((tm, tn), lambda i, j: (i, j)),
        compiler_params=_params(2),
        name="swiglu_in",
    )(h, w, w)


def _attn_kernel(slope_ref, sink_ref, q_ref, k_ref, v_ref, ybuf_ref, o_ref, *, seq):
    hkv = pl.program_id(1)
    span = 3 * CHUNK
    scale = HEAD_DIM ** -0.5

    def heads(lo):
        return [slice(lo + g * HEAD_DIM, lo + (g + 1) * HEAD_DIM) for g in range(ATT_GROUP)]

    def blocks(it, carry):
        q0s, vbs, scs = [], [], []
        for u in range(ATT_UNROLL):
            q0 = pl.multiple_of((it * ATT_UNROLL + u) * CHUNK, CHUNK)
            ks = pl.multiple_of(jnp.clip(q0 - CHUNK, 0, seq - span), CHUNK)
            kb = k_ref[pl.ds(ks, span), :].astype(BF16)
            vbs.append(v_ref[pl.ds(ks, span), :].astype(BF16))
            qs = jnp.concatenate([q_ref[pl.ds(q0, CHUNK), sl] for sl in heads(0)], axis=0).astype(BF16)
            scs.append((_dot_nt(qs, kb) * scale, ks))
            q0s.append(q0)
        ps, dens = [], []
        for u in range(ATT_UNROLL):
            sc_all, ks = scs[u]
            qpos = q0s[u] + lax.broadcasted_iota(jnp.int32, (CHUNK, span), 0)
            kpos = ks + lax.broadcasted_iota(jnp.int32, (CHUNK, span), 1)
            dist = jnp.abs(qpos - kpos)
            valid = dist <= ATT_WINDOW
            distf = dist.astype(F32)
            pg, dg = [], []
            for g in range(ATT_GROUP):
                head = hkv * ATT_GROUP + g
                sc = sc_all[g * CHUNK:(g + 1) * CHUNK] - slope_ref[head] * distf
                sc = jnp.where(valid, sc, -jnp.inf)
                sink = sink_ref[head]
                mx = jnp.maximum(jnp.max(sc, axis=-1, keepdims=True), sink)
                p = jnp.exp(sc - mx)
                dg.append(jnp.sum(p, axis=-1, keepdims=True) + jnp.exp(sink - mx))
                pg.append(p.astype(BF16))
            ps.append(jnp.concatenate(pg, axis=0))
            dens.append(dg)
        for u in range(ATT_UNROLL):
            o_all = _dot(ps[u], vbs[u])
            for g, sl in enumerate(heads(0)):
                o = o_all[g * CHUNK:(g + 1) * CHUNK] / dens[u][g]
                o_ref[pl.ds(q0s[u], CHUNK), sl] = o.astype(o_ref.dtype)
        return carry

    lax.fori_loop(0, seq // (CHUNK * ATT_UNROLL), blocks, 0)


_IN_PLACE = pl.BlockSpec(memory_space=pl.ANY)


def _attn_call(proj, slopes, sink, ybuf):
    b, s, _ = proj.shape
    assert s % (CHUNK * ATT_UNROLL) == 0 and s >= 3 * CHUNK, s
    qw = ATT_GROUP * HEAD_DIM
    smem = pl.BlockSpec(memory_space=pltpu.SMEM)
    return pl.pallas_call(
        functools.partial(_attn_kernel, seq=s),
        out_shape=jax.ShapeDtypeStruct(ybuf.shape, ybuf.dtype),
        grid=(b, ATT_KV_HEADS),
        in_specs=[smem, smem,
                  pl.BlockSpec((None, s, qw), lambda i, h: (i, 0, h)),
                  pl.BlockSpec((None, s, HEAD_DIM), lambda i, h: (i, 0, _OFF_AK + h)),
                  pl.BlockSpec((None, s, HEAD_DIM), lambda i, h: (i, 0, _OFF_AV + h)),
                  _IN_PLACE],
        out_specs=pl.BlockSpec((None, None, s, qw), lambda i, h: (0, i, 0, h)),
        input_output_aliases={5: 0},
        compiler_params=_params(2),
        name="windowed_gqa",
    )(slopes, sink, proj, proj, proj, ybuf)


def _ret_kernel(lg_ref, q_ref, k_ref, v_ref, g_ref, ybuf_ref, o_ref, sf_ref, *, seq):
    head = pl.program_id(1)
    lg = lg_ref[head]
    nc = seq // CHUNK
    c = CHUNK
    row = lax.broadcasted_iota(jnp.int32, (c, c), 0)
    col = lax.broadcasted_iota(jnp.int32, (c, c), 1)
    dmat = jnp.exp(lg * jnp.abs(row - col).astype(F32))
    pos = lax.broadcasted_iota(jnp.int32, (c, 1), 0).astype(F32)
    kdec_f = jnp.exp(lg * (c - 1.0 - pos))
    kdec_b = jnp.exp(lg * pos)
    qdec_f = jnp.exp(lg * (pos + 1.0))
    qdec_b = jnp.exp(lg * (c - pos))
    cdec = jnp.exp(lg * c)
    kscale = HEAD_DIM ** -0.5

    def fwd(n, st):
        r0 = pl.multiple_of(n * c, c)
        sf_ref[n] = st
        kc = k_ref[pl.ds(r0, c), :] * kscale
        vc = v_ref[pl.ds(r0, c), :].astype(BF16)
        return st * cdec + _dot_tn((kc * kdec_f).astype(BF16), vc)

    lax.fori_loop(0, nc, fwd, jnp.zeros((c, c), F32), unroll=RET_UNROLL)

    def bwd(i, st):
        n = nc - 1 - i
        r0 = pl.multiple_of(n * c, c)
        qc = q_ref[pl.ds(r0, c), :]
        kc = k_ref[pl.ds(r0, c), :] * kscale
        vc = v_ref[pl.ds(r0, c), :].astype(BF16)
        p = _dot_nt(qc.astype(BF16), kc.astype(BF16)) * dmat
        o = _dot(p.astype(BF16), vc)
        o = o + _dot((qc * qdec_f).astype(BF16), sf_ref[n].astype(BF16))
        o = o + _dot((qc * qdec_b).astype(BF16), st.astype(BF16))
        mu = jnp.mean(o, axis=-1, keepdims=True)
        oc = o - mu
        var = jnp.mean(oc * oc, axis=-1, keepdims=True)
        y = oc * lax.rsqrt(var + LN_EPS) * _silu(g_ref[pl.ds(r0, c), :])
        o_ref[pl.ds(r0, c), :] = y.astype(o_ref.dtype)
        return st * cdec + _dot_tn((kc * kdec_b).astype(BF16), vc)

    lax.fori_loop(0, nc, bwd, jnp.zeros((c, c), F32), unroll=RET_UNROLL)


def _ret_call(proj, log_gamma, ybuf):
    b, s, _ = proj.shape

    def col(off):
        return pl.BlockSpec((None, s, HEAD_DIM), lambda i, h: (i, 0, off + h))

    return pl.pallas_call(
        functools.partial(_ret_kernel, seq=s),
        out_shape=jax.ShapeDtypeStruct(ybuf.shape, ybuf.dtype),
        grid=(b, RET_HEADS),
        in_specs=[pl.BlockSpec(memory_space=pltpu.SMEM), col(_OFF_RQ), col(_OFF_RK), col(_OFF_RV), col(_OFF_RG),
                  _IN_PLACE],
        out_specs=pl.BlockSpec((None, None, s, HEAD_DIM), lambda i, h: (1, i, 0, h)),
        scratch_shapes=[pltpu.VMEM((s // CHUNK, CHUNK, CHUNK), F32)],
        input_output_aliases={5: 0},
        compiler_params=_params(2),
        name="retention",
    )(log_gamma, proj, proj, proj, proj, ybuf)


def _sgu_kernel(u0_ref, u1_ref, v0_ref, v1_ref, w_ref, b_ref, ybuf_ref, o_ref):
    half = u0_ref.shape[-1]
    v = jnp.concatenate([_gelu_tanh(v0_ref[...]), _gelu_tanh(v1_ref[...])], axis=-1)
    mu = jnp.mean(v, axis=-1, keepdims=True)
    vc = v - mu
    var = jnp.mean(vc * vc, axis=-1, keepdims=True)
    vn = (vc * lax.rsqrt(var + LN_EPS)).astype(BF16)
    nchunk = vn.shape[0] // CHUNK
    for g in range(SG_GROUPS):
        lo = g * HEAD_DIM
        vg = jnp.concatenate([vn[r * CHUNK:(r + 1) * CHUNK, lo:lo + HEAD_DIM] for r in range(nchunk)], axis=-1)
        mixed = _dot(w_ref[g], vg) + b_ref[:, g:g + 1]
        u_ref = u0_ref if lo < half else u1_ref
        ul = lo % half
        for r in range(nchunk):
            u = _gelu_tanh(u_ref[r * CHUNK:(r + 1) * CHUNK, ul:ul + HEAD_DIM])
            o_ref[r * CHUNK:(r + 1) * CHUNK, lo:lo + HEAD_DIM] = (
                u * mixed[:, r * HEAD_DIM:(r + 1) * HEAD_DIM]).astype(o_ref.dtype)


def _sgu_call(proj, sg_w, sg_bt, ybuf):
    b, s, _ = proj.shape
    half = SG_GROUPS * HEAD_DIM // 2
    hb = half // HEAD_DIM
    rows = _pick(s, (4 * CHUNK, 2 * CHUNK, CHUNK))

    def blk(off):
        return pl.BlockSpec((None, rows, half), lambda i, n: (i, n, off))

    return pl.pallas_call(
        _sgu_kernel,
        out_shape=jax.ShapeDtypeStruct(ybuf.shape, ybuf.dtype),
        grid=(b, s // rows),
        in_specs=[blk(_OFF_SU // hb), blk(_OFF_SU // hb + 1), blk(_OFF_SV // hb), blk(_OFF_SV // hb + 1),
                  pl.BlockSpec((SG_GROUPS, CHUNK, CHUNK), lambda i, n: (0, 0, 0)),
                  pl.BlockSpec((CHUNK, SG_GROUPS), lambda i, n: (0, 0)),
                  _IN_PLACE],
        out_specs=pl.BlockSpec((None, None, rows, BRANCH_W), lambda i, n: (2, i, n, 0)),
        input_output_aliases={6: 0},
        compiler_params=_params(2),
        name="spatial_gating",
    )(proj, proj, proj, proj, sg_w, sg_bt, ybuf)


def _split3(x):
    x1 = x.astype(BF16)
    r1 = x - x1.astype(F32)
    x2 = r1.astype(BF16)
    x3 = (r1 - x2.astype(F32)).astype(BF16)
    return x1, x2, x3


def _hgrn_tables(forward):
    c = CHUNK
    row = np.arange(c)[:, None]
    col = np.arange(c)[None, :]
    lmat = ((col <= row) if forward else (col >= row)).astype(np.float32)
    qms, lms, lsub = [], [], []
    b = c // 2
    while b >= SUB:
        upper_r = (row % (2 * b)) >= b
        upper_c = (col % (2 * b)) >= b
        q_side = upper_r if forward else ~upper_r
        k_side = ~upper_c if forward else upper_c
        pair = (q_side & k_side & ((row // (2 * b)) == (col // (2 * b)))).astype(np.float32)
        if b >= TILE_ROWS:
            lsub.append(pair[q_side[:, 0]])
        else:
            qms.append(np.broadcast_to(q_side, (c, HEAD_DIM)).astype(np.float32))
            lms.append(pair)
        b //= 2
    off = np.where((row // SUB) == (col // SUB), col - row, 2 * c).astype(np.int32)
    return lmat, np.stack(qms), np.stack(lms), off, np.stack(lsub)


def _hgrn_chunks(qzv, lb, st, tmat, qmask_ref, lmask_ref, lsub_ref, off3, forward):
    c = CHUNK
    tr = TILE_ROWS
    tiles = (c // tr, tr, c)
    gates = []
    for q, z, v in qzv:
        f = lb + (1.0 - lb) * _sigmoid(z)
        g1, g2, g3 = _split3(jnp.log2(f))
        a = _dot(tmat, g1) + _dot(tmat, g2) + _dot(tmat, g3)
        gates.append((1.0 - f, a))

    pairs = []
    for (q, z, v), (kk, a) in zip(qzv, gates):
        p_tiles = [None] * (c // tr)

        def add_rows(first_tile, block):
            for i in range(block.shape[0] // tr):
                t = first_tile + i
                x = block[tr * i:tr * (i + 1)]
                p_tiles[t] = x if p_tiles[t] is None else p_tiles[t] + x

        n_sliced = 0
        for li in range(_HGRN_LEVELS):
            b = c >> (li + 1)
            nb = c // (2 * b)
            a4 = a.reshape(nb, 2 * b, c)
            mid = a4[:, b - 1:b, :] if forward else a4[:, b:b + 1, :]
            if b >= tr:
                qs, ks = (slice(b, 2 * b), slice(0, b)) if forward else (slice(0, b), slice(b, 2 * b))
                q4, k4 = q.reshape(nb, 2 * b, c), kk.reshape(nb, 2 * b, c)
                qt = q4[:, qs, :] * jnp.exp2(a4[:, qs, :] - mid)
                kt = k4[:, ks, :] * jnp.exp2(mid - a4[:, ks, :])
                zero = jnp.zeros_like(kt)
                kt = jnp.concatenate([kt, zero] if forward else [zero, kt], axis=1)
                ps = _dot_nt(qt.reshape(c // 2, c).astype(BF16), kt.reshape(c, c).astype(BF16)) * lsub_ref[n_sliced]
                n_sliced += 1
                for blk in range(nb):
                    add_rows((blk * 2 * b + qs.start) // tr, ps[blk * b:(blk + 1) * b])
            else:
                lm = li - n_sliced
                d = (a4 - mid).reshape(c, c)
                u = ((kk + qmask_ref[lm] * (q - kk)) * jnp.exp2(-jnp.abs(d))).astype(BF16)
                add_rows(0, _dot_nt(u, u) * lmask_ref[lm])

        a3, k3, q3 = a.reshape(tiles), kk.reshape(tiles), q.reshape(tiles)
        p3 = jnp.stack(p_tiles)
        for delta in range(SUB):
            if delta == 0:
                w = q3 * k3
            else:
                sh = delta if forward else tr - delta
                a_s = pltpu.roll(a3, sh, 1)
                k_s = pltpu.roll(k3, sh, 1)
                w = q3 * k_s * jnp.exp2(jnp.minimum(a3 - a_s, 0.0))
            rs = jnp.sum(w, axis=-1, keepdims=True)
            p3 = jnp.where(off3 == (-delta if forward else delta), rs, p3)
        pairs.append(p3.reshape(c, c).astype(BF16))

    outs = []
    for (q, z, v), (kk, a), p in zip(qzv, gates, pairs):
        a_end = a[c - 1:c] if forward else a[0:1]
        vb = v.astype(BF16)
        o = _dot_nt((q * jnp.exp2(a)).astype(BF16), st.astype(BF16)) + _dot(p, vb)
        st = st * jnp.exp2(a_end) + _dot_tn(vb, (kk * jnp.exp2(a_end - a)).astype(BF16))
        outs.append(o)
    return outs, st


def _hgrn_kernel(q_ref, zf_ref, zb_ref, v_ref, g_ref, lb_ref, tmat_ref, qmask_ref, lmask_ref, lsub_ref, off_ref,
                 ybuf_ref, o_ref, of_ref, *, seq):
    nc = seq // CHUNK
    c = CHUNK
    lb = lb_ref[...]
    off3 = off_ref[...].reshape(c // TILE_ROWS, TILE_ROWS, c)

    def rows(n):
        return pl.ds(pl.multiple_of(n * c, c), c)

    def fwd(it, st):
        ns = [it * HGRN_UNROLL + u for u in range(HGRN_UNROLL)]
        qzv = [(q_ref[rows(n), :], zf_ref[rows(n), :], v_ref[rows(n), :]) for n in ns]
        outs, st = _hgrn_chunks(qzv, lb, st, tmat_ref[0], qmask_ref.at[0], lmask_ref.at[0], lsub_ref.at[0], off3, True)
        for n, o in zip(ns, outs):
            of_ref[rows(n), :] = o
        return st

    lax.fori_loop(0, nc // HGRN_UNROLL, fwd, jnp.zeros((c, c), F32))

    def bwd(it, st):
        ns = [nc - 1 - (it * HGRN_UNROLL + u) for u in range(HGRN_UNROLL)]
        qzv = [(q_ref[rows(n), :], zb_ref[rows(n), :], v_ref[rows(n), :]) for n in ns]
        outs, st = _hgrn_chunks(qzv, lb, st, tmat_ref[1], qmask_ref.at[1], lmask_ref.at[1], lsub_ref.at[1], off3,
                                False)
        for n, o in zip(ns, outs):
            o = o + of_ref[rows(n), :]
            y = o * lax.rsqrt(jnp.mean(o * o, axis=-1, keepdims=True) + LN_EPS) * _silu(g_ref[rows(n), :])
            o_ref[rows(n), :] = y.astype(o_ref.dtype)
        return st

    lax.fori_loop(0, nc // HGRN_UNROLL, bwd, jnp.zeros((c, c), F32))


def _hgrn_call(proj, lb, ybuf):
    b, s, _ = proj.shape
    assert s % (CHUNK * HGRN_UNROLL) == 0, s
    tf, tb = _hgrn_tables(True), _hgrn_tables(False)
    tmat = jnp.asarray(np.stack([tf[0], tb[0]]), BF16)
    qmask = jnp.asarray(np.stack([tf[1], tb[1]]))
    lmask = jnp.asarray(np.stack([tf[2], tb[2]]))
    lsub = jnp.asarray(np.stack([tf[4], tb[4]]))
    off = jnp.asarray(tf[3])

    def col(off_):
        return pl.BlockSpec((None, s, HEAD_DIM), lambda i, h: (i, 0, off_ + h))

    def whole(arr):
        return pl.BlockSpec(arr.shape, lambda i, h: (0,) * arr.ndim)

    return pl.pallas_call(
        functools.partial(_hgrn_kernel, seq=s),
        out_shape=jax.ShapeDtypeStruct(ybuf.shape, ybuf.dtype),
        grid=(b, HGRN_HEADS),
        in_specs=[col(_OFF_DQ), col(_OFF_DFF), col(_OFF_DFB), col(_OFF_DI), col(_OFF_DG),
                  pl.BlockSpec((1, HEAD_DIM), lambda i, h: (0, h)),
                  whole(tmat), whole(qmask), whole(lmask), whole(lsub), whole(off), _IN_PLACE],
        out_specs=pl.BlockSpec((None, None, s, HEAD_DIM), lambda i, h: (3, i, 0, h)),
        scratch_shapes=[pltpu.VMEM((s, HEAD_DIM), F32)],
        input_output_aliases={11: 0},
        compiler_params=_params(2),
        name="hgrn2",
    )(proj, proj, proj, proj, proj, lb, tmat, qmask, lmask, lsub, off, ybuf)


def kernel(x, c, w_in, attn_sink, sg_w, sg_b, hgrn_lb_logits, w_branch, w_gate, w_o, w_mod, b_mod, ln_g, ln_b, w_ffn_in, w_ffn_out):
    bsz, seq, d = x.shape
    depth = w_in.shape[0]
    m = bsz * seq
    alpha = (2.0 * depth) ** 0.25

    pz = jax.nn.softmax(hgrn_lb_logits.astype(F32), axis=0)
    lower_bounds = jnp.cumsum(pz, axis=0) - pz[:1]
    slopes = jnp.exp2(-8.0 * jnp.arange(1, ATT_HEADS + 1, dtype=F32) / ATT_HEADS)
    log_gamma = jnp.log1p(-jnp.exp2(-5.0 - jnp.arange(RET_HEADS, dtype=F32)))

    assert bsz <= TILE_ROWS, bsz
    c_pad = jnp.zeros((TILE_ROWS, d), F32).at[:bsz].set(c)
    mod = _mod_call(c_pad, w_mod, b_mod)[:, :bsz]

    def mod_part(l, i):
        return mod[l, :, i * d:(i + 1) * d].reshape(bsz, 1, d)

    tm = _pick(m, (1024, 512, 256, 128))
    tn = _pick(d, (512, 256, 128))
    h = _modulate_call(x, mod_part(0, 1), mod_part(0, 0))
    ys = jnp.zeros((N_BRANCH, bsz, seq, BRANCH_W), BF16)
    for l in range(depth):
        h2d = h.reshape(m, d)
        proj = _matmul(h2d, w_in, l, tm=tm, tn=_pick(w_in.shape[2], (512, 256, 128)),
                       out_dtype=F32, name="in_proj").reshape(bsz, seq, -1)
        ys = _attn_call(proj, slopes, attn_sink[l].astype(F32), ys)
        ys = _ret_call(proj, log_gamma, ys)
        ys = _sgu_call(proj, sg_w[l].astype(BF16), sg_b[l].T, ys)
        ys = _hgrn_call(proj, lower_bounds[l].reshape(1, -1), ys)
        merged = _merge_call(h2d, ys.reshape(N_BRANCH, m, BRANCH_W), w_gate, w_branch, l)
        mix = _matmul(merged, w_o, l, tm=tm, tn=tn, out_dtype=BF16, name="out_proj").reshape(bsz, seq, d)
        x, h = _ln_call(x, mix, mod_part(l, 2), ln_g[l, 0:1], ln_b[l, 0:1], alpha, mod_part(l, 4), mod_part(l, 3))
        act = _ffn_in_call(h.reshape(m, d), w_ffn_in, l, _pick(m, (2048, 1024, 512, 256, 128)))
        ffn = _matmul(act, w_ffn_out, l, tm=tm, tn=_pick(d, (256, 128)), out_dtype=BF16, name="ffn_out",
                      lhs_buffers=1).reshape(bsz, seq, d)
        if l + 1 < depth:
            x, h = _ln_call(x, ffn, mod_part(l, 5), ln_g[l, 1:2], ln_b[l, 1:2], alpha,
                            mod_part(l + 1, 1), mod_part(l + 1, 0))
        else:
            x, _ = _ln_call(x, ffn, mod_part(l, 5), ln_g[l, 1:2], ln_b[l, 1:2], alpha)
    return x
```

```python
import functools

import jax
import jax.numpy as jnp
import numpy as np
from jax import lax
from jax.experimental import pallas as pl
from jax.experimental.pallas import tpu as pltpu

F32 = jnp.float32
BF16 = jnp.bfloat16

HEAD_DIM = 128
ATT_HEADS = 8
ATT_KV_HEADS = 2
ATT_GROUP = ATT_HEADS // ATT_KV_HEADS
ATT_WINDOW = 128
RET_HEADS = 8
SG_GROUPS = 8
HGRN_HEADS = 8
CHUNK = 128
TILE_ROWS = 8
SUB = 4
_HGRN_LEVELS = (CHUNK // SUB).bit_length() - 1
HGRN_UNROLL = 8
HGRN_HEADS_PER_STEP = 2
RET_UNROLL = 8
ATT_UNROLL = 8
BRANCH_W = 8 * HEAD_DIM
N_BRANCH = 4
LN_EPS = 1e-5

_OFF_AQ = 0
_OFF_AK = _OFF_AQ + ATT_HEADS
_OFF_AV = _OFF_AK + ATT_KV_HEADS
_OFF_RQ = _OFF_AV + ATT_KV_HEADS
_OFF_RK = _OFF_RQ + RET_HEADS
_OFF_RV = _OFF_RK + RET_HEADS
_OFF_RG = _OFF_RV + RET_HEADS
_OFF_SU = _OFF_RG + RET_HEADS
_OFF_SV = _OFF_SU + SG_GROUPS
_OFF_DQ = _OFF_SV + SG_GROUPS
_OFF_DFF = _OFF_DQ + HGRN_HEADS
_OFF_DFB = _OFF_DFF + HGRN_HEADS
_OFF_DI = _OFF_DFB + HGRN_HEADS
_OFF_DG = _OFF_DI + HGRN_HEADS

VMEM_LIMIT_BYTES_V7X = 56 * 1024 * 1024


def _params(n_axes):
    return pltpu.CompilerParams(dimension_semantics=("arbitrary",) * n_axes,
                                vmem_limit_bytes=VMEM_LIMIT_BYTES_V7X)


def _dot(a, b):
    return jnp.dot(a, b, preferred_element_type=F32)


def _dot_nt(a, b):
    return lax.dot_general(a, b, (((1,), (1,)), ((), ())), preferred_element_type=F32)


def _dot_tn(a, b):
    return lax.dot_general(a, b, (((0,), (0,)), ((), ())), preferred_element_type=F32)


def _sigmoid(x):
    return 1.0 / (1.0 + jnp.exp(-x))


def _silu(x):
    return x * _sigmoid(x)


def _gelu_tanh(x):
    return 0.5 * x * (1.0 + jnp.tanh(0.7978845608028654 * (x + 0.044715 * (x * x * x))))


def _pick(n, prefs):
    for p in prefs:
        if n % p == 0:
            return p
    return n


def _mod_kernel(c_ref, w_ref, b_ref, o_ref):
    ca = _silu(c_ref[...]).astype(BF16)
    o_ref[...] = _dot(ca, w_ref[...].astype(BF16)) + b_ref[...]


def _mod_call(c_pad, w_mod, b_mod):
    depth, d, n = w_mod.shape
    rows = c_pad.shape[0]
    tn = _pick(n, (512, 256, 128))
    return pl.pallas_call(
        _mod_kernel,
        out_shape=jax.ShapeDtypeStruct((depth, rows, n), F32),
        grid=(depth, n // tn),
        in_specs=[pl.BlockSpec((rows, d), lambda l, j: (0, 0)),
                  pl.BlockSpec((None, d, tn), lambda l, j: (l, 0, j)),
                  pl.BlockSpec((None, 1, tn), lambda l, j: (l, 0, j))],
        out_specs=pl.BlockSpec((None, rows, tn), lambda l, j: (l, 0, j)),
        compiler_params=_params(2),
        name="adaln_mod",
    )(c_pad, w_mod, b_mod.reshape(depth, 1, n))


def _modulate_kernel(x_ref, sc_ref, sh_ref, o_ref):
    o_ref[...] = (x_ref[...] * (1.0 + sc_ref[...]) + sh_ref[...]).astype(o_ref.dtype)


def _modulate_call(x, sc, sh):
    b, s, d = x.shape
    ts = _pick(s, (512, 256, 128))
    return pl.pallas_call(
        _modulate_kernel,
        out_shape=jax.ShapeDtypeStruct((b, s, d), BF16),
        grid=(b, s // ts),
        in_specs=[pl.BlockSpec((None, ts, d), lambda i, j: (i, j, 0)),
                  pl.BlockSpec((None, 1, d), lambda i, j: (i, 0, 0)),
                  pl.BlockSpec((None, 1, d), lambda i, j: (i, 0, 0))],
        out_specs=pl.BlockSpec((None, ts, d), lambda i, j: (i, j, 0)),
        compiler_params=_params(2),
        name="modulate",
    )(x, sc, sh)


def _mm_kernel(x_ref, w_ref, o_ref):
    o_ref[...] = _dot(x_ref[...], w_ref[...].astype(BF16)).astype(o_ref.dtype)


def _matmul(x, w, layer, *, tm, tn, out_dtype, name, lhs_buffers=2):
    m, k = x.shape
    n = w.shape[2]
    return pl.pallas_call(
        _mm_kernel,
        out_shape=jax.ShapeDtypeStruct((m, n), out_dtype),
        grid=(m // tm, n // tn),
        in_specs=[pl.BlockSpec((tm, k), lambda i, j: (i, 0), pipeline_mode=pl.Buffered(lhs_buffers)),
                  pl.BlockSpec((None, k, tn), lambda i, j: (layer, 0, j))],
        out_specs=pl.BlockSpec((tm, tn), lambda i, j: (i, j)),
        compiler_params=_params(2),
        name=name,
    )(x, w)


def _merge_kernel(h_ref, y_ref, wg_ref, wb_ref, o_ref, acc_ref):
    @pl.when(pl.program_id(2) == 0)
    def _():
        acc_ref[...] = jnp.zeros_like(acc_ref)

    gate = _sigmoid(_dot(h_ref[...], wg_ref[...].astype(BF16)))
    acc = acc_ref[...] + gate * _dot(y_ref[...], wb_ref[...].astype(BF16))
    acc_ref[...] = acc
    o_ref[...] = acc.astype(o_ref.dtype)


def _merge_call(h, ys, wg, wb, layer):
    m, d = h.shape
    nb, _, bw = ys.shape
    n = wg.shape[3]
    tm = _pick(m, (1024, 512, 256, 128))
    tn = _pick(n, (512, 256, 128))
    return pl.pallas_call(
        _merge_kernel,
        out_shape=jax.ShapeDtypeStruct((m, n), BF16),
        grid=(m // tm, n // tn, nb),
        in_specs=[pl.BlockSpec((tm, d), lambda i, j, r: (i, 0)),
                  pl.BlockSpec((None, tm, bw), lambda i, j, r: (r, i, 0)),
                  pl.BlockSpec((None, None, d, tn), lambda i, j, r: (layer, r, 0, j)),
                  pl.BlockSpec((None, None, bw, tn), lambda i, j, r: (layer, r, 0, j))],
        out_specs=pl.BlockSpec((tm, tn), lambda i, j, r: (i, j)),
        scratch_shapes=[pltpu.VMEM((tm, tn), F32)],
        compiler_params=_params(3),
        name="gated_merge",
    )(h, ys, wg, wb)


def _residual_ln(x, m, gate, lg, lb, alpha):
    z = alpha * x + (1.0 + gate) * m
    mu = jnp.mean(z, axis=-1, keepdims=True)
    zc = z - mu
    var = jnp.mean(zc * zc, axis=-1, keepdims=True)
    return zc * lax.rsqrt(var + LN_EPS) * lg + lb


def _ln_mod_kernel(x_ref, m_ref, gate_ref, lg_ref, lb_ref, sc_ref, sh_ref, xo_ref, ho_ref, *, alpha):
    xn = _residual_ln(x_ref[...], m_ref[...], gate_ref[...], lg_ref[...], lb_ref[...], alpha)
    xo_ref[...] = xn
    ho_ref[...] = (xn * (1.0 + sc_ref[...]) + sh_ref[...]).astype(ho_ref.dtype)


def _ln_kernel(x_ref, m_ref, gate_ref, lg_ref, lb_ref, xo_ref, *, alpha):
    xo_ref[...] = _residual_ln(x_ref[...], m_ref[...], gate_ref[...], lg_ref[...], lb_ref[...], alpha)


def _ln_call(x, m, gate, ln_g, ln_b, alpha, sc=None, sh=None):
    b, s, d = x.shape
    ts = _pick(s, (256, 128))
    big = pl.BlockSpec((None, ts, d), lambda i, j: (i, j, 0))
    per_b = pl.BlockSpec((None, 1, d), lambda i, j: (i, 0, 0))
    shared = pl.BlockSpec((1, d), lambda i, j: (0, 0))
    x_shape = jax.ShapeDtypeStruct((b, s, d), F32)
    if sc is None:
        return pl.pallas_call(
            functools.partial(_ln_kernel, alpha=alpha),
            out_shape=x_shape,
            grid=(b, s // ts),
            in_specs=[big, big, per_b, shared, shared],
            out_specs=big,
            compiler_params=_params(2),
            name="residual_ln_last",
        )(x, m, gate, ln_g, ln_b), None
    return pl.pallas_call(
        functools.partial(_ln_mod_kernel, alpha=alpha),
        out_shape=(x_shape, jax.ShapeDtypeStruct((b, s, d), BF16)),
        grid=(b, s // ts),
        in_specs=[big, big, per_b, shared, shared, per_b, per_b],
        out_specs=(big, big),
        compiler_params=_params(2),
        name="residual_ln",
    )(x, m, gate, ln_g, ln_b, sc, sh)


def _ffn_in_kernel(h_ref, wa_ref, wg_ref, o_ref):
    h = h_ref[...]
    a = _dot(h, wa_ref[...].astype(BF16))
    g = _dot(h, wg_ref[...].astype(BF16))
    o_ref[...] = (_silu(a) * g).astype(o_ref.dtype)


def _ffn_in_call(h, w, layer, tm):
    m, d = h.shape
    f = w.shape[2] // 2
    tn = _pick(f, (256, 128))
    nj = f // tn
    return pl.pallas_call(
        _ffn_in_kernel,
        out_shape=jax.ShapeDtypeStruct((m, f), BF16),
        grid=(m // tm, nj),
        in_specs=[pl.BlockSpec((tm, d), lambda i, j: (i, 0)),
                  pl.BlockSpec((None, d, tn), lambda i, j: (layer, 0, j)),
                  pl.BlockSpec((None, d, tn), lambda i, j: (layer, 0, j + nj))],
        out_specs=pl.BlockSpec((tm, tn), lambda i, j: (i, j)),
        compiler_params=_params(2),
        name="swiglu_in",
    )(h, w, w)


def _attn_kernel(slope_ref, sink_ref, q_ref, k_ref, v_ref, ybuf_ref, o_ref, *, seq):
    hkv = pl.program_id(1)
    span = 3 * CHUNK
    scale = HEAD_DIM ** -0.5

    def heads(lo):
        return [slice(lo + g * HEAD_DIM, lo + (g + 1) * HEAD_DIM) for g in range(ATT_GROUP)]

    def blocks(it, carry):
        q0s, vbs, scs = [], [], []
        for u in range(ATT_UNROLL):
            q0 = pl.multiple_of((it * ATT_UNROLL + u) * CHUNK, CHUNK)
            ks = pl.multiple_of(jnp.clip(q0 - CHUNK, 0, seq - span), CHUNK)
            kb = k_ref[pl.ds(ks, span), :].astype(BF16)
            vbs.append(v_ref[pl.ds(ks, span), :].astype(BF16))
            qs = jnp.concatenate([q_ref[pl.ds(q0, CHUNK), sl] for sl in heads(0)], axis=0).astype(BF16)
            scs.append((_dot_nt(qs, kb) * scale, ks))
            q0s.append(q0)
        ps, dens = [], []
        for u in range(ATT_UNROLL):
            sc_all, ks = scs[u]
            qpos = q0s[u] + lax.broadcasted_iota(jnp.int32, (CHUNK, span), 0)
            kpos = ks + lax.broadcasted_iota(jnp.int32, (CHUNK, span), 1)
            dist = jnp.abs(qpos - kpos)
            valid = dist <= ATT_WINDOW
            distf = dist.astype(F32)
            pg, dg = [], []
            for g in range(ATT_GROUP):
                head = hkv * ATT_GROUP + g
                sc = sc_all[g * CHUNK:(g + 1) * CHUNK] - slope_ref[head] * distf
                sc = jnp.where(valid, sc, -jnp.inf)
                sink = sink_ref[head]
                mx = jnp.maximum(jnp.max(sc, axis=-1, keepdims=True), sink)
                p = jnp.exp(sc - mx)
                dg.append(jnp.sum(p, axis=-1, keepdims=True) + jnp.exp(sink - mx))
                pg.append(p.astype(BF16))
            ps.append(jnp.concatenate(pg, axis=0))
            dens.append(dg)
        for u in range(ATT_UNROLL):
            o_all = _dot(ps[u], vbs[u])
            for g, sl in enumerate(heads(0)):
                o = o_all[g * CHUNK:(g + 1) * CHUNK] / dens[u][g]
                o_ref[pl.ds(q0s[u], CHUNK), sl] = o.astype(o_ref.dtype)
        return carry

    lax.fori_loop(0, seq // (CHUNK * ATT_UNROLL), blocks, 0)


_IN_PLACE = pl.BlockSpec(memory_space=pl.ANY)


def _attn_call(proj, slopes, sink, ybuf):
    b, s, _ = proj.shape
    assert s % (CHUNK * ATT_UNROLL) == 0 and s >= 3 * CHUNK, s
    qw = ATT_GROUP * HEAD_DIM
    smem = pl.BlockSpec(memory_space=pltpu.SMEM)
    return pl.pallas_call(
        functools.partial(_attn_kernel, seq=s),
        out_shape=jax.ShapeDtypeStruct(ybuf.shape, ybuf.dtype),
        grid=(b, ATT_KV_HEADS),
        in_specs=[smem, smem,
                  pl.BlockSpec((None, s, qw), lambda i, h: (i, 0, h)),
                  pl.BlockSpec((None, s, HEAD_DIM), lambda i, h: (i, 0, _OFF_AK + h)),
                  pl.BlockSpec((None, s, HEAD_DIM), lambda i, h: (i, 0, _OFF_AV + h)),
                  _IN_PLACE],
        out_specs=pl.BlockSpec((None, None, s, qw), lambda i, h: (0, i, 0, h)),
        input_output_aliases={5: 0},
        compiler_params=_params(2),
        name="windowed_gqa",
    )(slopes, sink, proj, proj, proj, ybuf)


def _ret_kernel(lg_ref, q_ref, k_ref, v_ref, g_ref, ybuf_ref, o_ref, sf_ref, *, seq):
    head = pl.program_id(1)
    lg = lg_ref[head]
    nc = seq // CHUNK
    c = CHUNK
    row = lax.broadcasted_iota(jnp.int32, (c, c), 0)
    col = lax.broadcasted_iota(jnp.int32, (c, c), 1)
    dmat = jnp.exp(lg * jnp.abs(row - col).astype(F32))
    pos = lax.broadcasted_iota(jnp.int32, (c, 1), 0).astype(F32)
    kdec_f = jnp.exp(lg * (c - 1.0 - pos))
    kdec_b = jnp.exp(lg * pos)
    qdec_f = jnp.exp(lg * (pos + 1.0))
    qdec_b = jnp.exp(lg * (c - pos))
    cdec = jnp.exp(lg * c)
    kscale = HEAD_DIM ** -0.5

    def fwd(n, st):
        r0 = pl.multiple_of(n * c, c)
        sf_ref[n] = st
        kc = k_ref[pl.ds(r0, c), :] * kscale
        vc = v_ref[pl.ds(r0, c), :].astype(BF16)
        return st * cdec + _dot_tn((kc * kdec_f).astype(BF16), vc)

    lax.fori_loop(0, nc, fwd, jnp.zeros((c, c), F32), unroll=RET_UNROLL)

    def bwd(i, st):
        n = nc - 1 - i
        r0 = pl.multiple_of(n * c, c)
        qc = q_ref[pl.ds(r0, c), :]
        kc = k_ref[pl.ds(r0, c), :] * kscale
        vc = v_ref[pl.ds(r0, c), :].astype(BF16)
        p = _dot_nt(qc.astype(BF16), kc.astype(BF16)) * dmat
        o = _dot(p.astype(BF16), vc)
        o = o + _dot((qc * qdec_f).astype(BF16), sf_ref[n].astype(BF16))
        o = o + _dot((qc * qdec_b).astype(BF16), st.astype(BF16))
        mu = jnp.mean(o, axis=-1, keepdims=True)
        oc = o - mu
        var = jnp.mean(oc * oc, axis=-1, keepdims=True)
        y = oc * lax.rsqrt(var + LN_EPS) * _silu(g_ref[pl.ds(r0, c), :])
        o_ref[pl.ds(r0, c), :] = y.astype(o_ref.dtype)
        return st * cdec + _dot_tn((kc * kdec_b).astype(BF16), vc)

    lax.fori_loop(0, nc, bwd, jnp.zeros((c, c), F32), unroll=RET_UNROLL)


def _ret_call(proj, log_gamma, ybuf):
    b, s, _ = proj.shape

    def col(off):
        return pl.BlockSpec((None, s, HEAD_DIM), lambda i, h: (i, 0, off + h))

    return pl.pallas_call(
        functools.partial(_ret_kernel, seq=s),
        out_shape=jax.ShapeDtypeStruct(ybuf.shape, ybuf.dtype),
        grid=(b, RET_HEADS),
        in_specs=[pl.BlockSpec(memory_space=pltpu.SMEM), col(_OFF_RQ), col(_OFF_RK), col(_OFF_RV), col(_OFF_RG),
                  _IN_PLACE],
        out_specs=pl.BlockSpec((None, None, s, HEAD_DIM), lambda i, h: (1, i, 0, h)),
        scratch_shapes=[pltpu.VMEM((s // CHUNK, CHUNK, CHUNK), F32)],
        input_output_aliases={5: 0},
        compiler_params=_params(2),
        name="retention",
    )(log_gamma, proj, proj, proj, proj, ybuf)


def _sgu_kernel(u0_ref, u1_ref, v0_ref, v1_ref, w_ref, b_ref, ybuf_ref, o_ref):
    half = u0_ref.shape[-1]
    v = jnp.concatenate([_gelu_tanh(v0_ref[...]), _gelu_tanh(v1_ref[...])], axis=-1)
    mu = jnp.mean(v, axis=-1, keepdims=True)
    vc = v - mu
    var = jnp.mean(vc * vc, axis=-1, keepdims=True)
    vn = (vc * lax.rsqrt(var + LN_EPS)).astype(BF16)
    nchunk = vn.shape[0] // CHUNK
    for g in range(SG_GROUPS):
        lo = g * HEAD_DIM
        vg = jnp.concatenate([vn[r * CHUNK:(r + 1) * CHUNK, lo:lo + HEAD_DIM] for r in range(nchunk)], axis=-1)
        mixed = _dot(w_ref[g], vg) + b_ref[:, g:g + 1]
        u_ref = u0_ref if lo < half else u1_ref
        ul = lo % half
        for r in range(nchunk):
            u = _gelu_tanh(u_ref[r * CHUNK:(r + 1) * CHUNK, ul:ul + HEAD_DIM])
            o_ref[r * CHUNK:(r + 1) * CHUNK, lo:lo + HEAD_DIM] = (
                u * mixed[:, r * HEAD_DIM:(r + 1) * HEAD_DIM]).astype(o_ref.dtype)


def _sgu_call(proj, sg_w, sg_bt, ybuf):
    b, s, _ = proj.shape
    half = SG_GROUPS * HEAD_DIM // 2
    hb = half // HEAD_DIM
    rows = _pick(s, (4 * CHUNK, 2 * CHUNK, CHUNK))

    def blk(off):
        return pl.BlockSpec((None, rows, half), lambda i, n: (i, n, off))

    return pl.pallas_call(
        _sgu_kernel,
        out_shape=jax.ShapeDtypeStruct(ybuf.shape, ybuf.dtype),
        grid=(b, s // rows),
        in_specs=[blk(_OFF_SU // hb), blk(_OFF_SU // hb + 1), blk(_OFF_SV // hb), blk(_OFF_SV // hb + 1),
                  pl.BlockSpec((SG_GROUPS, CHUNK, CHUNK), lambda i, n: (0, 0, 0)),
                  pl.BlockSpec((CHUNK, SG_GROUPS), lambda i, n: (0, 0)),
                  _IN_PLACE],
        out_specs=pl.BlockSpec((None, None, rows, BRANCH_W), lambda i, n: (2, i, n, 0)),
        input_output_aliases={6: 0},
        compiler_params=_params(2),
        name="spatial_gating",
    )(proj, proj, proj, proj, sg_w, sg_bt, ybuf)


def _split3(x):
    x1 = x.astype(BF16)
    r1 = x - x1.astype(F32)
    x2 = r1.astype(BF16)
    x3 = (r1 - x2.astype(F32)).astype(BF16)
    return x1, x2, x3


def _hgrn_tables(forward):
    c = CHUNK
    row = np.arange(c)[:, None]
    col = np.arange(c)[None, :]
    lmat = ((col <= row) if forward else (col >= row)).astype(np.float32)
    qms, lms, lsub = [], [], []
    b = c // 2
    while b >= SUB:
        upper_r = (row % (2 * b)) >= b
        upper_c = (col % (2 * b)) >= b
        q_side = upper_r if forward else ~upper_r
        k_side = ~upper_c if forward else upper_c
        pair = (q_side & k_side & ((row // (2 * b)) == (col // (2 * b)))).astype(np.float32)
        if b >= TILE_ROWS:
            lsub.append(pair[q_side[:, 0]])
        else:
            qms.append(np.broadcast_to(q_side, (c, HEAD_DIM)).astype(np.float32))
            lms.append(pair)
        b //= 2
    off = np.where((row // SUB) == (col // SUB), col - row, 2 * c).astype(np.int32)
    return lmat, np.stack(qms), np.stack(lms), off, np.stack(lsub)


def _hgrn_chunks(qzv, lb, st, tmat, qmask_ref, lmask_ref, lsub_ref, off3, forward):
    c = CHUNK
    tr = TILE_ROWS
    tiles = (c // tr, tr, c)
    gates = []
    for q, z, v in qzv:
        f = lb + (1.0 - lb) * _sigmoid(z)
        g1, g2, g3 = _split3(jnp.log2(f))
        a = _dot(tmat, g1) + _dot(tmat, g2) + _dot(tmat, g3)
        gates.append((1.0 - f, a))

    pairs = []
    for (q, z, v), (kk, a) in zip(qzv, gates):
        p_tiles = [None] * (c // tr)

        def add_rows(first_tile, block):
            for i in range(block.shape[0] // tr):
                t = first_tile + i
                x = block[tr * i:tr * (i + 1)]
                p_tiles[t] = x if p_tiles[t] is None else p_tiles[t] + x

        n_sliced = 0
        for li in range(_HGRN_LEVELS):
            b = c >> (li + 1)
            nb = c // (2 * b)
            a4 = a.reshape(nb, 2 * b, c)
            mid = a4[:, b - 1:b, :] if forward else a4[:, b:b + 1, :]
            if b >= tr:
                qs, ks = (slice(b, 2 * b), slice(0, b)) if forward else (slice(0, b), slice(b, 2 * b))
                q4, k4 = q.reshape(nb, 2 * b, c), kk.reshape(nb, 2 * b, c)
                qt = q4[:, qs, :] * jnp.exp2(a4[:, qs, :] - mid)
                kt = k4[:, ks, :] * jnp.exp2(mid - a4[:, ks, :])
                zero = jnp.zeros_like(kt)
                kt = jnp.concatenate([kt, zero] if forward else [zero, kt], axis=1)
                ps = _dot_nt(qt.reshape(c // 2, c).astype(BF16), kt.reshape(c, c).astype(BF16)) * lsub_ref[n_sliced]
                n_sliced += 1
                for blk in range(nb):
                    add_rows((blk * 2 * b + qs.start) // tr, ps[blk * b:(blk + 1) * b])
            else:
                lm = li - n_sliced
                d = (a4 - mid).reshape(c, c)
                u = ((kk + qmask_ref[lm] * (q - kk)) * jnp.exp2(-jnp.abs(d))).astype(BF16)
                add_rows(0, _dot_nt(u, u) * lmask_ref[lm])

        a3, k3, q3 = a.reshape(tiles), kk.reshape(tiles), q.reshape(tiles)
        p3 = jnp.stack(p_tiles)
        for delta in range(SUB):
            if delta == 0:
                w = q3 * k3
            else:
                sh = delta if forward else tr - delta
                a_s = pltpu.roll(a3, sh, 1)
                k_s = pltpu.roll(k3, sh, 1)
                w = q3 * k_s * jnp.exp2(jnp.minimum(a3 - a_s, 0.0))
            rs = jnp.sum(w, axis=-1, keepdims=True)
            p3 = jnp.where(off3 == (-delta if forward else delta), rs, p3)
        pairs.append(p3.reshape(c, c).astype(BF16))

    outs = []
    for (q, z, v), (kk, a), p in zip(qzv, gates, pairs):
        a_end = a[c - 1:c] if forward else a[0:1]
        vb = v.astype(BF16)
        o = _dot_nt((q * jnp.exp2(a)).astype(BF16), st.astype(BF16)) + _dot(p, vb)
        st = st * jnp.exp2(a_end) + _dot_tn(vb, (kk * jnp.exp2(a_end - a)).astype(BF16))
        outs.append(o)
    return outs, st


def _hgrn_kernel(q_ref, zf_ref, zb_ref, v_ref, g_ref, lb_ref, tmat_ref, qmask_ref, lmask_ref, lsub_ref, off_ref,
                 ybuf_ref, o_ref, of_ref, *, seq):
    nc = seq // CHUNK
    c = CHUNK
    off3 = off_ref[...].reshape(c // TILE_ROWS, TILE_ROWS, c)

    def rows(n):
        return pl.ds(pl.multiple_of(n * c, c), c)

    for hh in range(HGRN_HEADS_PER_STEP):
        cs = slice(hh * HEAD_DIM, (hh + 1) * HEAD_DIM)
        lb = lb_ref[:, cs]

        def fwd(it, st, cs=cs, lb=lb):
            ns = [it * HGRN_UNROLL + u for u in range(HGRN_UNROLL)]
            qzv = [(q_ref[rows(n), cs], zf_ref[rows(n), cs], v_ref[rows(n), cs]) for n in ns]
            outs, st = _hgrn_chunks(qzv, lb, st, tmat_ref[0], qmask_ref.at[0], lmask_ref.at[0], lsub_ref.at[0],
                                    off3, True)
            for n, o in zip(ns, outs):
                of_ref[rows(n), :] = o
            return st

        lax.fori_loop(0, nc // HGRN_UNROLL, fwd, jnp.zeros((c, c), F32))

        def bwd(it, st, cs=cs, lb=lb):
            ns = [nc - 1 - (it * HGRN_UNROLL + u) for u in range(HGRN_UNROLL)]
            qzv = [(q_ref[rows(n), cs], zb_ref[rows(n), cs], v_ref[rows(n), cs]) for n in ns]
            outs, st = _hgrn_chunks(qzv, lb, st, tmat_ref[1], qmask_ref.at[1], lmask_ref.at[1], lsub_ref.at[1],
                                    off3, False)
            for n, o in zip(ns, outs):
                o = o + of_ref[rows(n), :]
                y = o * lax.rsqrt(jnp.mean(o * o, axis=-1, keepdims=True) + LN_EPS) * _silu(g_ref[rows(n), cs])
                o_ref[rows(n), cs] = y.astype(o_ref.dtype)
            return st

        lax.fori_loop(0, nc // HGRN_UNROLL, bwd, jnp.zeros((c, c), F32))


def _hgrn_call(proj, lb, ybuf):
    b, s, _ = proj.shape
    assert s % (CHUNK * HGRN_UNROLL) == 0, s
    tf, tb = _hgrn_tables(True), _hgrn_tables(False)
    tmat = jnp.asarray(np.stack([tf[0], tb[0]]), BF16)
    qmask = jnp.asarray(np.stack([tf[1], tb[1]]))
    lmask = jnp.asarray(np.stack([tf[2], tb[2]]))
    lsub = jnp.asarray(np.stack([tf[4], tb[4]]))
    off = jnp.asarray(tf[3])

    hps = HGRN_HEADS_PER_STEP
    width = hps * HEAD_DIM

    def col(off_):
        assert off_ % hps == 0
        return pl.BlockSpec((None, s, width), lambda i, h: (i, 0, off_ // hps + h))

    def whole(arr):
        return pl.BlockSpec(arr.shape, lambda i, h: (0,) * arr.ndim)

    return pl.pallas_call(
        functools.partial(_hgrn_kernel, seq=s),
        out_shape=jax.ShapeDtypeStruct(ybuf.shape, ybuf.dtype),
        grid=(b, HGRN_HEADS // hps),
        in_specs=[col(_OFF_DQ), col(_OFF_DFF), col(_OFF_DFB), col(_OFF_DI), col(_OFF_DG),
                  pl.BlockSpec((1, width), lambda i, h: (0, h)),
                  whole(tmat), whole(qmask), whole(lmask), whole(lsub), whole(off), _IN_PLACE],
        out_specs=pl.BlockSpec((None, None, s, width), lambda i, h: (3, i, 0, h)),
        scratch_shapes=[pltpu.VMEM((s, HEAD_DIM), F32)],
        input_output_aliases={11: 0},
        compiler_params=_params(2),
        name="hgrn2",
    )(proj, proj, proj, proj, proj, lb, tmat, qmask, lmask, lsub, off, ybuf)


def kernel(x, c, w_in, attn_sink, sg_w, sg_b, hgrn_lb_logits, w_branch, w_gate, w_o, w_mod, b_mod, ln_g, ln_b, w_ffn_in, w_ffn_out):
    bsz, seq, d = x.shape
    depth = w_in.shape[0]
    m = bsz * seq
    alpha = (2.0 * depth) ** 0.25

    pz = jax.nn.softmax(hgrn_lb_logits.astype(F32), axis=0)
    lower_bounds = jnp.cumsum(pz, axis=0) - pz[:1]
    slopes = jnp.exp2(-8.0 * jnp.arange(1, ATT_HEADS + 1, dtype=F32) / ATT_HEADS)
    log_gamma = jnp.log1p(-jnp.exp2(-5.0 - jnp.arange(RET_HEADS, dtype=F32)))

    assert bsz <= TILE_ROWS, bsz
    c_pad = jnp.zeros((TILE_ROWS, d), F32).at[:bsz].set(c)
    mod = _mod_call(c_pad, w_mod, b_mod)[:, :bsz]

    def mod_part(l, i):
        return mod[l, :, i * d:(i + 1) * d].reshape(bsz, 1, d)

    tm = _pick(m, (1024, 512, 256, 128))
    tn = _pick(d, (512, 256, 128))
    h = _modulate_call(x, mod_part(0, 1), mod_part(0, 0))
    ys = jnp.zeros((N_BRANCH, bsz, seq, BRANCH_W), BF16)
    for l in range(depth):
        h2d = h.reshape(m, d)
        proj = _matmul(h2d, w_in, l, tm=tm, tn=_pick(w_in.shape[2], (512, 256, 128)),
                       out_dtype=F32, name="in_proj").reshape(bsz, seq, -1)
        ys = _attn_call(proj, slopes, attn_sink[l].astype(F32), ys)
        ys = _ret_call(proj, log_gamma, ys)
        ys = _sgu_call(proj, sg_w[l].astype(BF16), sg_b[l].T, ys)
        ys = _hgrn_call(proj, lower_bounds[l].reshape(1, -1), ys)
        merged = _merge_call(h2d, ys.reshape(N_BRANCH, m, BRANCH_W), w_gate, w_branch, l)
        mix = _matmul(merged, w_o, l, tm=tm, tn=tn, out_dtype=BF16, name="out_proj").reshape(bsz, seq, d)
        x, h = _ln_call(x, mix, mod_part(l, 2), ln_g[l, 0:1], ln_b[l, 0:1], alpha, mod_part(l, 4), mod_part(l, 3))
        act = _ffn_in_call(h.reshape(m, d), w_ffn_in, l, _pick(m, (2048, 1024, 512, 256, 128)))
        ffn = _matmul(act, w_ffn_out, l, tm=tm, tn=_pick(d, (256, 128)), out_dtype=BF16, name="ffn_out",
                      lhs_buffers=1).reshape(bsz, seq, d)
        if l + 1 < depth:
            x, h = _ln_call(x, ffn, mod_part(l, 5), ln_g[l, 1:2], ln_b[l, 1:2], alpha,
                            mod_part(l + 1, 1), mod_part(l + 1, 0))
        else:
            x, _ = _ln_call(x, ffn, mod_part(l, 5), ln_g[l, 1:2], ln_b[l, 1:2], alpha)
    return x
```
